```python
import math
import jax
import jax.numpy as jnp
from jax import lax
import numpy as np

D_MODEL = 1024
BATCH = 8
SEQ = 2048
DEPTH = 2
DEC_BATCH = 16
DEC_SEQ = 2048
PAST_LEN = 128

GLA_HEADS = 4
GLA_DK = 64
GLA_DV = 128
GLA_KW = GLA_HEADS * GLA_DK
GLA_WIDTH = GLA_HEADS * GLA_DV
GLA_GATE_RANK = 16
GLA_GATE_TEMP = 16.0
GLA_CHUNK = 64

HY_WIDTH = D_MODEL - GLA_WIDTH
HY_ORDER = 2
HY_DIRS = 2
HY_SHORT = 3
HY_BANDS = 8
HY_EMB = 1 + 2 * HY_BANDS
HY_FFN = 64
HY_N_FILT = HY_ORDER * HY_DIRS * HY_WIDTH
HY_FAST_DECAY = 0.3
HY_SLOW_DECAY = 1.5
HY_DECAY_TARGET = 1e-2

D_FF = -(-8 * D_MODEL // (3 * 256)) * 256
NORM_EPS = 1e-6

IN_SIZES = (GLA_KW, GLA_KW, GLA_WIDTH, GLA_WIDTH, 2 * GLA_GATE_RANK, (HY_ORDER + 1) * HY_WIDTH)
IN_WIDTH = sum(IN_SIZES)

kernel_name = "hybrid_gla_hyena_encoder"


def rmsnorm(x, w):
    xf = x.astype(jnp.float32)
    y = xf * lax.rsqrt(jnp.mean(xf * xf, axis=-1, keepdims=True) + NORM_EPS)
    return (y * w.astype(jnp.float32)).astype(x.dtype)


def gla_chunk_scan(q, k, v, g):
    B, H, T, DK = q.shape
    DV = v.shape[-1]
    C = GLA_CHUNK
    NC = T // C

    def chunks(a):
        return jnp.moveaxis(a.reshape(B, H, NC, C, a.shape[-1]), 2, 0)

    qc, kc, vc, gc = chunks(q), chunks(k), chunks(v), chunks(g)
    bc = jnp.cumsum(gc, axis=-2)
    lower = jnp.tril(jnp.ones((C, C), dtype=bool))[:, :, None]

    def step(S, inp):
        qi, ki, vi, bi = inp
        o_inter = jnp.einsum('bhik,bhkv->bhiv', qi * jnp.exp(bi), S)
        diff = jnp.where(lower, bi[:, :, :, None, :] - bi[:, :, None, :, :], -jnp.inf)
        A = jnp.einsum('bhik,bhjk,bhijk->bhij', qi, ki, jnp.exp(diff))
        o = o_inter + jnp.einsum('bhij,bhjv->bhiv', A, vi)
        b_last = bi[:, :, -1:, :]
        S = jnp.exp(b_last[:, :, 0, :, None]) * S + jnp.einsum(
            'bhjk,bhjv->bhkv', ki * jnp.exp(b_last - bi), vi)
        return S, o

    S0 = jnp.zeros((B, H, DK, DV), jnp.float32)
    _, o = lax.scan(step, S0, (qc, kc, vc, bc))
    return jnp.moveaxis(o, 0, 2).reshape(B, H, T, DV)


def gla_group(q, k, v, r, g_lr, wg_f, bg_f, wg_b, bg_b, norm_w):
    B, T, _ = q.shape

    def heads(a, d):
        return a.reshape(B, T, GLA_HEADS, d).transpose(0, 2, 1, 3).astype(jnp.float32)

    qh = heads(q, GLA_DK) * (GLA_DK ** -0.5)
    kh = heads(k, GLA_DK)
    vh = heads(v, GLA_DV)
    lr_f, lr_b = jnp.split(g_lr, 2, axis=-1)

    def log_gate(z, w, b):
        return jax.nn.log_sigmoid(heads(z @ w + b, GLA_DK)) / GLA_GATE_TEMP

    g_f = log_gate(lr_f, wg_f, bg_f)
    g_b = log_gate(lr_b, wg_b, bg_b)
    rev = lambda a: jnp.flip(a, axis=2)
    o = gla_chunk_scan(qh, kh, vh, g_f) + rev(gla_chunk_scan(rev(qh), rev(kh), rev(vh), rev(g_b)))
    o = rmsnorm(o.transpose(0, 2, 1, 3), norm_w)
    return o.reshape(B, T, GLA_WIDTH).astype(r.dtype) * jax.nn.silu(r)


def short_conv_centred(u, w, b):
    T = u.shape[1]
    up = jnp.pad(u, ((0, 0), (1, 1), (0, 0)))
    return up[:, :T] * w[0] + up[:, 1:T + 1] * w[1] + up[:, 2:] * w[2] + b


def hyena_filters(L, w1, b1, freq1, w2, b2, freq2, w3):
    t = jnp.linspace(0.0, 1.0, L, dtype=jnp.float32)[:, None]
    pos_w = (2.0 * math.pi / L) * jnp.arange(L, dtype=jnp.float32)[:, None]
    bands = jnp.linspace(1e-4, HY_BANDS - 1.0, HY_BANDS, dtype=jnp.float32)
    feats = jnp.concatenate([t, jnp.cos(bands * pos_w), -jnp.sin(bands * pos_w)], axis=-1)
    feats = feats.astype(w1.dtype)
    hid = jnp.sin(freq1 * (feats @ w1 + b1))
    hid = jnp.sin(freq2 * (hid @ w2 + b2))
    filt = (hid @ w3).astype(jnp.float32).reshape(L, HY_ORDER, HY_DIRS, HY_WIDTH)
    max_decay = math.log(HY_DECAY_TARGET) / HY_FAST_DECAY
    min_decay = math.log(HY_DECAY_TARGET) / HY_SLOW_DECAY
    deltas = jnp.abs(jnp.linspace(min_decay, max_decay, HY_WIDTH, dtype=jnp.float32))
    filt = filt * jnp.exp(-t * deltas)[:, None, None, :]
    fwd, bwd = filt[:, :, 0], filt[:, :, 1]
    kern = jnp.concatenate([fwd, jnp.zeros_like(fwd[:1]), jnp.flip(bwd[1:], axis=0)], axis=0)
    return jnp.fft.rfft(kern, axis=0)


def fft_long_conv(u, k_freq, d):
    L = u.shape[1]
    uf = u.astype(jnp.float32)
    y = jnp.fft.irfft(jnp.fft.rfft(uf, n=2 * L, axis=1) * k_freq, n=2 * L, axis=1)[:, :L]
    return (y + uf * d.astype(jnp.float32)).astype(u.dtype)


def hyena_group(u, conv_w, conv_b, w1, b1, freq1, w2, b2, freq2, w3, skip, norm_w):
    u = short_conv_centred(u, conv_w, conv_b)
    v, x1, x2 = jnp.split(u, HY_ORDER + 1, axis=-1)
    k_freq = hyena_filters(u.shape[1], w1, b1, freq1, w2, b2, freq2, w3)
    z = x1 * fft_long_conv(v, k_freq[:, 0], skip[0])
    z = x2 * fft_long_conv(z, k_freq[:, 1], skip[1])
    return rmsnorm(z, norm_w)


def encoder_layer(x, norm_mix, w_in, gla_wg_f, gla_bg_f, gla_wg_b, gla_bg_b, gla_norm,
                  hy_conv_w, hy_conv_b, hy_w1, hy_b1, hy_freq1, hy_w2, hy_b2, hy_freq2, hy_w3,
                  hy_skip, hy_norm, w_out, norm_ffn, w_gate, w_up, w_down):
    h = rmsnorm(x, norm_mix)
    proj = jnp.einsum('btd,de->bte', h, w_in)
    q, k, v, r, g_lr, hy_in = jnp.split(proj, np.cumsum(IN_SIZES)[:-1].tolist(), axis=-1)
    o_gla = gla_group(q, k, v, r, g_lr, gla_wg_f, gla_bg_f, gla_wg_b, gla_bg_b, gla_norm)
    o_hy = hyena_group(hy_in, hy_conv_w, hy_conv_b, hy_w1, hy_b1, hy_freq1, hy_w2, hy_b2,
                       hy_freq2, hy_w3, hy_skip, hy_norm)
    x = x + jnp.einsum('bte,ed->btd', jnp.concatenate([o_gla, o_hy], axis=-1), w_out)
    h = rmsnorm(x, norm_ffn)
    ff = jax.nn.silu(h @ w_gate) * (h @ w_up)
    return x + ff @ w_down


def encoder_trunk(x, weights, norm_final):
    for layer in range(DEPTH):
        x = encoder_layer(x, *[w[layer] for w in weights])
    return rmsnorm(x, norm_final)


def setup_inputs(seed: int = 0) -> dict:
    key = jax.random.key(seed)
    keys = jax.random.split(key, 26)

    def nrm(i, shape, scale):
        return scale * jax.random.normal(keys[i], shape, jnp.float32)

    L = DEPTH
    return {
        'x_prompt': nrm(0, (BATCH, SEQ, D_MODEL), 1.0),
        'x_sample': nrm(1, (DEC_BATCH, DEC_SEQ, D_MODEL), 1.0),
        'norm_mix': 1.0 + nrm(2, (L, D_MODEL), 0.02),
        'w_in': nrm(3, (L, D_MODEL, IN_WIDTH), D_MODEL ** -0.5),
        'gla_wg_f': nrm(4, (L, GLA_GATE_RANK, GLA_KW), GLA_GATE_RANK ** -0.5),
        'gla_bg_f': nrm(5, (L, GLA_KW), 0.1),
        'gla_wg_b': nrm(6, (L, GLA_GATE_RANK, GLA_KW), GLA_GATE_RANK ** -0.5),
        'gla_bg_b': nrm(7, (L, GLA_KW), 0.1),
        'gla_norm': 1.0 + nrm(8, (L, GLA_DV), 0.02),
        'hy_conv_w': nrm(9, (L, HY_SHORT, (HY_ORDER + 1) * HY_WIDTH), 0.5),
        'hy_conv_b': nrm(10, (L, (HY_ORDER + 1) * HY_WIDTH), 0.02),
        'hy_w1': nrm(11, (L, HY_EMB, HY_FFN), HY_EMB ** -0.5),
        'hy_b1': nrm(12, (L, HY_FFN), 0.1),
        'hy_freq1': 1.0 + nrm(13, (L, HY_FFN), 0.02),
        'hy_w2': nrm(14, (L, HY_FFN, HY_FFN), HY_FFN ** -0.5),
        'hy_b2': nrm(15, (L, HY_FFN), 0.1),
        'hy_freq2': 1.0 + nrm(16, (L, HY_FFN), 0.02),
        'hy_w3': nrm(17, (L, HY_FFN, HY_N_FILT), 0.01),
        'hy_skip': nrm(18, (L, HY_ORDER, HY_WIDTH), 0.1),
        'hy_norm': 1.0 + nrm(19, (L, HY_WIDTH), 0.02),
        'w_out': nrm(20, (L, D_MODEL, D_MODEL), D_MODEL ** -0.5),
        'norm_ffn': 1.0 + nrm(21, (L, D_MODEL), 0.02),
        'w_gate': nrm(22, (L, D_MODEL, D_FF), D_MODEL ** -0.5),
        'w_up': nrm(23, (L, D_MODEL, D_FF), D_MODEL ** -0.5),
        'w_down': nrm(24, (L, D_FF, D_MODEL), D_FF ** -0.5),
        'norm_final': 1.0 + nrm(25, (D_MODEL,), 0.02),
    }


def reference(x_prompt, x_sample, norm_mix, w_in, gla_wg_f, gla_bg_f, gla_wg_b, gla_bg_b,
              gla_norm, hy_conv_w, hy_conv_b, hy_w1, hy_b1, hy_freq1, hy_w2, hy_b2, hy_freq2,
              hy_w3, hy_skip, hy_norm, w_out, norm_ffn, w_gate, w_up, w_down, norm_final):
    weights = (norm_mix, w_in, gla_wg_f, gla_bg_f, gla_wg_b, gla_bg_b, gla_norm,
               hy_conv_w, hy_conv_b, hy_w1, hy_b1, hy_freq1, hy_w2, hy_b2, hy_freq2, hy_w3,
               hy_skip, hy_norm, w_out, norm_ffn, w_gate, w_up, w_down)
    y_prompt = encoder_trunk(x_prompt, weights, norm_final)
    y_sample = encoder_trunk(x_sample, weights, norm_final)
    return (y_prompt, y_sample)
```

```python
import functools
import math

import numpy as np
import jax
import jax.numpy as jnp
from jax import lax
from jax.experimental import pallas as pl
from jax.experimental.pallas import tpu as pltpu

f32 = jnp.float32
bf16 = jnp.bfloat16

LANES = 128
VMEM_LIMIT_BYTES = 56 * 1024 * 1024

NORM_EPS = 1e-6

GLA_HEADS = 4
GLA_DK = 64
GLA_DV = 128
GLA_KW = GLA_HEADS * GLA_DK
GLA_WIDTH = GLA_HEADS * GLA_DV
GLA_GATE_RANK = 16
GLA_GATE_TEMP = 16.0
GLA_CHUNK = 64
GLA_SUB = 16

HY_ORDER = 2
HY_DIRS = 2
HY_BANDS = 8
HY_EMB = 1 + 2 * HY_BANDS
HY_FFN = 64
HY_FAST_DECAY = 0.3
HY_SLOW_DECAY = 1.5
HY_DECAY_TARGET = 1e-2
HY_FTILE = 512
HY_CBLK = 256
HY_BBLK = 2


def _cparams(*sem):
    return pltpu.CompilerParams(dimension_semantics=sem, vmem_limit_bytes=VMEM_LIMIT_BYTES)


def _const_spec(shape):
    nd = len(shape)
    return pl.BlockSpec(shape, lambda *_: (0,) * nd, pipeline_mode=pl.Buffered(1))


def _split_bf16(a):
    hi = a.astype(bf16)
    lo = (a - hi.astype(f32)).astype(bf16)
    return hi, lo


def _dot(a, b):
    return jnp.dot(a, b, preferred_element_type=f32)


def _dot_nt(a, b):
    return lax.dot_general(a, b, (((1,), (1,)), ((), ())), preferred_element_type=f32)


def _dot_tn(a, b):
    return lax.dot_general(a, b, (((0,), (0,)), ((), ())), preferred_element_type=f32)


def _dot3(a_hi, a_lo, b):
    b_hi, b_lo = _split_bf16(b)
    return _dot(a_hi, b_hi) + (_dot(a_lo, b_hi) + _dot(a_hi, b_lo))


def _dot2(a, w):
    a_hi, a_lo = _split_bf16(a)
    return _dot(a_hi, w) + _dot(a_lo, w)


def _rms(x):
    return x * lax.rsqrt(jnp.mean(x * x, axis=-1, keepdims=True) + NORM_EPS)


def _silu(x):
    return x / (1.0 + jnp.exp(-x))


def _dft_kernel(o_ref, *, n_fft, rows):
    half = n_fft // 2
    i = pl.program_id(0)
    n = lax.broadcasted_iota(jnp.int32, (rows, half), 0) + i * rows
    f = lax.broadcasted_iota(jnp.int32, (rows, half), 1)
    ang = ((n * f) & (n_fft - 1)).astype(f32) * (2.0 * math.pi / n_fft)
    o_ref[:, :half] = jnp.cos(ang).astype(o_ref.dtype)
    nyq = jnp.where((n & 1) == 0, 1.0, -1.0)
    im = jnp.where(f == 0, nyq, -jnp.sin(ang))
    o_ref[:, half:] = im.astype(o_ref.dtype)


def _dft_matrix(seq):
    rows = 256
    return pl.pallas_call(
        functools.partial(_dft_kernel, n_fft=2 * seq, rows=rows),
        grid=(seq // rows,),
        out_specs=pl.BlockSpec((rows, 2 * seq), lambda i: (i, 0)),
        out_shape=jax.ShapeDtypeStruct((seq, 2 * seq), bf16),
        compiler_params=_cparams("parallel"),
        name="dft_matrix",
    )()


def _filter_kernel(feats_ref, w1h_ref, w1l_ref, b1_ref, fr1_ref, w2h_ref, w2l_ref, b2_ref, fr2_ref,
                   w3fh_ref, w3fl_ref, w3bh_ref, w3bl_ref, delta_ref, skip_ref, wf_ref,
                   a_ref, b_ref, corr_ref, *, seq):
    n_fft = 2 * seq
    feats = feats_ref[...]
    hid = jnp.sin(fr1_ref[...] * (_dot3(w1h_ref[...], w1l_ref[...], feats) + b1_ref[...]))
    hid = jnp.sin(fr2_ref[...] * (_dot3(w2h_ref[...], w2l_ref[...], hid) + b2_ref[...]))
    fwd = _dot3(w3fh_ref[0], w3fl_ref[0], hid)
    bwd = _dot3(w3bh_ref[0], w3bl_ref[0], hid)
    lag = lax.broadcasted_iota(jnp.int32, fwd.shape, 1)
    t = lag.astype(f32) * (1.0 / (seq - 1))
    window = jnp.exp(-t * delta_ref[...])
    fwd = fwd * window
    bwd = jnp.where(lag == 0, 0.0, bwd * window)
    sym = fwd + bwd
    k_re = _dot2(sym, wf_ref[:, :seq])
    k_im = _dot2(fwd - bwd, wf_ref[:, seq:])
    k_ny = _dot2(sym, wf_ref[:, seq:seq + LANES])[:, 0:1]
    skip = skip_ref[0]
    fcol = lax.broadcasted_iota(jnp.int32, k_re.shape, 1)
    dc = fcol == 0
    a = (k_re + skip) * jnp.where(dc, 1.0 / n_fft, 2.0 / n_fft)
    a_ref[0] = a
    b_ref[0] = jnp.where(dc, 0.0, k_im * (2.0 / n_fft))
    lane = lax.broadcasted_iota(jnp.int32, (a.shape[0], LANES), 1)
    corr_ref[0] = jnp.where(lane == 0, (k_ny + skip) * (1.0 / n_fft) - a[:, 0:1], 0.0)


def _hyena_filters(feats, w1, b1, fr1, w2, b2, fr2, w3, skip, wf, *, seq, width):
    pad = LANES

    def pad2(m, r, c):
        return jnp.zeros((r, c), f32).at[:m.shape[0], :m.shape[1]].set(m)

    w1h, w1l = _split_bf16(pad2(w1.T, pad, pad))
    w2h, w2l = _split_bf16(pad2(w2.T, pad, pad))
    w3t = pad2(w3.T, w3.shape[1], pad).reshape(HY_ORDER, HY_DIRS, width, pad)
    w3fh, w3fl = _split_bf16(w3t[:, 0])
    w3bh, w3bl = _split_bf16(w3t[:, 1])
    col = lambda v: pad2(v[:, None], pad, 1)
    max_decay = math.log(HY_DECAY_TARGET) / HY_FAST_DECAY
    min_decay = math.log(HY_DECAY_TARGET) / HY_SLOW_DECAY
    delta = jnp.abs(jnp.linspace(min_decay, max_decay, width, dtype=f32))[:, None]
    cb = HY_CBLK
    full = lambda shape: pl.BlockSpec(shape, lambda o, c: (0,) * len(shape))
    w3spec = pl.BlockSpec((1, cb, pad), lambda o, c: (o, c, 0))
    out_spec = pl.BlockSpec((1, cb, seq), lambda o, c: (o, c, 0))
    return pl.pallas_call(
        functools.partial(_filter_kernel, seq=seq),
        grid=(HY_ORDER, width // cb),
        in_specs=[full((pad, seq)), full((pad, pad)), full((pad, pad)), full((pad, 1)), full((pad, 1)),
                  full((pad, pad)), full((pad, pad)), full((pad, 1)), full((pad, 1)),
                  w3spec, w3spec, w3spec, w3spec,
                  pl.BlockSpec((cb, 1), lambda o, c: (c, 0)),
                  pl.BlockSpec((1, cb, 1), lambda o, c: (o, c, 0)),
                  _const_spec((seq, 2 * seq))],
        out_specs=[out_spec, out_spec, pl.BlockSpec((1, cb, LANES), lambda o, c: (o, c, 0))],
        out_shape=[jax.ShapeDtypeStruct((HY_ORDER, width, seq), f32),
                   jax.ShapeDtypeStruct((HY_ORDER, width, seq), f32),
                   jax.ShapeDtypeStruct((HY_ORDER, width, LANES), f32)],
        compiler_params=_cparams("arbitrary", "arbitrary"),
        name="hyena_filters",
    )(feats, w1h, w1l, col(b1), col(fr1), w2h, w2l, col(b2), col(fr2),
      w3fh, w3fl, w3bh, w3bl, delta, skip[:, :, None], wf)


def _positional_features(seq):
    t = jnp.linspace(0.0, 1.0, seq, dtype=f32)[:, None]
    pos_w = (2.0 * math.pi / seq) * jnp.arange(seq, dtype=f32)[:, None]
    bands = jnp.linspace(1e-4, HY_BANDS - 1.0, HY_BANDS, dtype=f32)
    feats = jnp.concatenate([t, jnp.cos(bands * pos_w), -jnp.sin(bands * pos_w)], axis=-1)
    return jnp.zeros((LANES, seq), f32).at[:HY_EMB].set(feats.T)


def _in_proj_kernel(x_ref, nw_ref, wa_ref, wlr_ref, wt_ref,
                    q_ref, k_ref, v_ref, r_ref, x2_ref, lr_ref, vt_ref):
    h = (_rms(x_ref[0]) * nw_ref[...]).astype(bf16)
    pa = _dot(h, wa_ref[...])
    o = 0
    for ref in (q_ref, k_ref, v_ref, r_ref, x2_ref):
        w = ref.shape[-1]
        ref[0] = pa[:, o:o + w].astype(ref.dtype)
        o += w
    lr_ref[0] = _dot(h, wlr_ref[...])
    vt_ref[0] = _dot_nt(wt_ref[...], h).astype(vt_ref.dtype)


def _in_proj(x, nw, wa, wlr, wt, *, tm):
    bsz, seq, d = x.shape
    hw = wt.shape[0] // 2
    row = lambda w: pl.BlockSpec((1, tm, w), lambda b, t: (b, t, 0))
    return pl.pallas_call(
        _in_proj_kernel,
        grid=(bsz, seq // tm),
        in_specs=[row(d), _const_spec(nw.shape), _const_spec(wa.shape), _const_spec(wlr.shape),
                  _const_spec(wt.shape)],
        out_specs=[row(GLA_KW), row(GLA_KW), row(GLA_WIDTH), row(GLA_WIDTH), row(hw), row(LANES),
                   pl.BlockSpec((1, 2 * hw, tm), lambda b, t: (b, 0, t))],
        out_shape=[jax.ShapeDtypeStruct((bsz, seq, GLA_KW), bf16),
                   jax.ShapeDtypeStruct((bsz, seq, GLA_KW), bf16),
                   jax.ShapeDtypeStruct((bsz, seq, GLA_WIDTH), bf16),
                   jax.ShapeDtypeStruct((bsz, seq, GLA_WIDTH), bf16),
                   jax.ShapeDtypeStruct((bsz, seq, hw), bf16),
                   jax.ShapeDtypeStruct((bsz, seq, LANES), f32),
                   jax.ShapeDtypeStruct((bsz, 2 * hw, seq), bf16)],
        compiler_params=_cparams("parallel", "parallel"),
        name="in_proj",
    )(x, nw, wa, wlr, wt)


def _gla_constants(rev):
    c, s, h, dk, dv = GLA_CHUNK, GLA_SUB, GLA_HEADS, GLA_DK, GLA_DV
    i = np.arange(c)[:, None]
    j = np.arange(c)[None, :]
    if not rev:
        tri = j <= i
        ref1 = np.broadcast_to(j <= c // 2 - 1, (c, c))
        ref2 = j <= (i // (c // 2)) * (c // 2) + s - 1
        up1 = (i >= c // 2)
        up2 = (i % (c // 2)) >= s
    else:
        tri = j >= i
        ref1 = np.broadcast_to(j >= c // 2, (c, c))
        ref2 = j >= (i // (c // 2)) * (c // 2) + s
        up1 = (i < c // 2)
        up2 = (i % (c // 2)) < s
    cum = np.concatenate([tri, ref1, ref2], axis=0).astype(np.float32)
    lane_j = np.arange(h * c)[None, :] % c
    same32 = ((i // (c // 2)) == (lane_j // (c // 2))).astype(np.float32)
    lag = np.zeros((s, c, h * c), np.float32)
    for d in range(s):
        if not rev:
            lag[d] = (lane_j == i - d) & (i % s >= d)
        else:
            lag[d] = (lane_j == i + d) & (i % s + d <= s - 1)
    rows = np.concatenate([up1, up2], axis=1).astype(np.float32)
    return cum, rows, same32, lag


def _gla_chunk(t0, rev, q_ref, k_ref, v_ref, g_ref, st_ref, o_ref, kpad_ref, gpad_ref, p_ref,
               cum_ref, rows_ref, same32_ref, lag_ref, bd_k_ref, bd_v_ref, bd_s_ref, ones_bd_ref):
    c, s = GLA_CHUNK, GLA_SUB
    q = q_ref[0, pl.ds(t0, c), :].astype(f32)
    k = k_ref[0, pl.ds(t0, c), :].astype(f32)
    v = v_ref[0, pl.ds(t0, c), :]
    g = g_ref[pl.ds(t0, c), :]
    g_hi, g_lo = _split_bf16(g)
    cum = cum_ref[...]
    b3 = _dot(cum, g_hi) + _dot(cum, g_lo)
    b, r1, r2 = b3[:c], b3[c:2 * c], b3[2 * c:]
    b_end = b[0:1] if rev else b[c - 1:c]

    def tile_heads(x):
        return jnp.concatenate([x] * GLA_HEADS, axis=0)

    up1 = rows_ref[:, 0:1] > 0.5
    up2 = rows_ref[:, 1:2] > 0.5
    e1 = jnp.exp(jnp.minimum(jnp.where(up1, b - r1, r1 - b), 0.0))
    e2 = jnp.exp(jnp.minimum(jnp.where(up2, b - r2, r2 - b), 0.0))
    zero = jnp.zeros_like(q)
    bd_k = bd_k_ref[...]
    q1 = jnp.where(up1, q * e1, zero).astype(bf16)
    k1 = (tile_heads(jnp.where(up1, zero, k * e1)) * bd_k).astype(bf16)
    q2 = jnp.where(up2, q * e2, zero).astype(bf16)
    k2 = (tile_heads(jnp.where(up2, zero, k * e2)) * bd_k).astype(bf16)
    a = _dot_nt(q1, k1) + _dot_nt(q2, k2) * same32_ref[...]

    gam = jnp.exp(g)
    if rev:
        kpad_ref[0:c, :] = k
        gpad_ref[0:c, :] = gam
    else:
        kpad_ref[s:s + c, :] = k
        gpad_ref[s:s + c, :] = gam
    w = None
    for d in range(s):
        lo = d if rev else s - d
        kd = kpad_ref[lo:lo + c, :]
        if d == 0:
            pd = q * kd
        else:
            lo_g = d - 1 if rev else s - (d - 1)
            gd = gpad_ref[lo_g:lo_g + c, :]
            w = gd if w is None else w * gd
            pd = q * kd * w
        p_ref[d * c:(d + 1) * c, :] = pd.astype(bf16)
    red = _dot(p_ref[...], ones_bd_ref[...])
    for d in range(s):
        a = a + red[d * c:(d + 1) * c, :] * lag_ref[d]

    st = st_ref[...]
    vbd = tile_heads(v) * bd_v_ref[...]
    qs = (q * jnp.exp(b)).astype(bf16)
    o_ref[pl.ds(t0, c), :] = _dot(a.astype(bf16), vbd) + _dot_nt(qs, st.astype(bf16))

    kt = (k * jnp.exp(b_end - b)).astype(bf16)
    st_ref[...] = st * jnp.exp(b_end) + _dot_tn(v, kt) * bd_s_ref[...]


def _gla_kernel(q_ref, k_ref, v_ref, r_ref, lr_ref, wgh_ref, wgl_ref, bg_ref, nw_ref,
                cumf_ref, rowsf_ref, s32f_ref, lagf_ref, cumb_ref, rowsb_ref, s32b_ref, lagb_ref,
                bd_k_ref, bd_v_ref, bd_s_ref, ones_bd_ref,
                out_ref,
                gf_ref, gb_ref, of_ref, ob_ref, stf_ref, stb_ref,
                kpf_ref, gpf_ref, kpb_ref, gpb_ref, pf_ref, pb_ref, *, seq, rows):
    c = GLA_CHUNK
    kw = GLA_KW

    def gate_rows(i, carry):
        r0 = pl.multiple_of(i * rows, rows)
        lr_hi, lr_lo = _split_bf16(lr_ref[0, pl.ds(r0, rows), :])
        wgh = wgh_ref[...]
        z = _dot(lr_hi, wgh) + (_dot(lr_lo, wgh) + _dot(lr_hi, wgl_ref[...])) + bg_ref[...]
        g = (jnp.minimum(z, 0.0) - jnp.log(1.0 + jnp.exp(-jnp.abs(z)))) * (1.0 / GLA_GATE_TEMP)
        gf_ref[pl.ds(r0, rows), :] = g[:, :kw]
        gb_ref[pl.ds(r0, rows), :] = g[:, kw:]
        return carry

    lax.fori_loop(0, seq // rows, gate_rows, 0)

    stf_ref[...] = jnp.zeros_like(stf_ref)
    stb_ref[...] = jnp.zeros_like(stb_ref)
    for ref in (kpf_ref, gpf_ref, kpb_ref, gpb_ref):
        ref[...] = jnp.zeros_like(ref)

    consts = (bd_k_ref, bd_v_ref, bd_s_ref, ones_bd_ref)
    n_chunks = seq // c

    def scan(i, carry):
        tf = pl.multiple_of(i * c, c)
        tb = pl.multiple_of((n_chunks - 1 - i) * c, c)
        _gla_chunk(tf, False, q_ref, k_ref, v_ref, gf_ref, stf_ref, of_ref, kpf_ref, gpf_ref, pf_ref,
                   cumf_ref, rowsf_ref, s32f_ref, lagf_ref, *consts)
        _gla_chunk(tb, True, q_ref, k_ref, v_ref, gb_ref, stb_ref, ob_ref, kpb_ref, gpb_ref, pb_ref,
                   cumb_ref, rowsb_ref, s32b_ref, lagb_ref, *consts)
        return carry

    lax.fori_loop(0, n_chunks, scan, 0)

    def finish(i, carry):
        r0 = pl.multiple_of(i * rows, rows)
        o = of_ref[pl.ds(r0, rows), :] + ob_ref[pl.ds(r0, rows), :]
        gate = _silu(r_ref[0, pl.ds(r0, rows), :].astype(f32))
        nw = nw_ref[...]
        for h in range(GLA_HEADS):
            sl = slice(h * GLA_DV, (h + 1) * GLA_DV)
            out_ref[0, pl.ds(r0, rows), sl] = (_rms(o[:, sl]) * nw * gate[:, sl]).astype(out_ref.dtype)
        return carry

    lax.fori_loop(0, seq // rows, finish, 0)


def _gla(q, k, v, r, lr, wgh, wgl, bg, nw):
    bsz, seq, _ = q.shape
    c, s, h = GLA_CHUNK, GLA_SUB, GLA_HEADS
    cf = _gla_constants(False)
    cb = _gla_constants(True)

    def block_diag(nr, nc):
        rr = np.arange(h * nr)[:, None] // nr
        cc = np.arange(h * nc)[None, :] // nc
        return (rr == cc).astype(np.float32)

    bd_k = block_diag(c, GLA_DK)
    bd_v = block_diag(c, GLA_DV)
    bd_s = block_diag(GLA_DV, GLA_DK)
    ones_bd = block_diag(GLA_DK, c)
    consts = [jnp.asarray(cf[0], bf16), jnp.asarray(cf[1]), jnp.asarray(cf[2]), jnp.asarray(cf[3]),
              jnp.asarray(cb[0], bf16), jnp.asarray(cb[1]), jnp.asarray(cb[2]), jnp.asarray(cb[3]),
              jnp.asarray(bd_k), jnp.asarray(bd_v, bf16), jnp.asarray(bd_s), jnp.asarray(ones_bd, bf16)]
    per_b = lambda w: pl.BlockSpec((1, seq, w), lambda b: (b, 0, 0))
    params = [wgh, wgl, bg, nw]
    return pl.pallas_call(
        functools.partial(_gla_kernel, seq=seq, rows=256),
        grid=(bsz,),
        in_specs=[per_b(GLA_KW), per_b(GLA_KW), per_b(GLA_WIDTH), per_b(GLA_WIDTH), per_b(LANES)]
        + [_const_spec(p.shape) for p in params] + [_const_spec(x.shape) for x in consts],
        out_specs=per_b(GLA_WIDTH),
        out_shape=jax.ShapeDtypeStruct((bsz, seq, GLA_WIDTH), bf16),
        scratch_shapes=[pltpu.VMEM((seq, GLA_KW), f32), pltpu.VMEM((seq, GLA_KW), f32),
                        pltpu.VMEM((seq, GLA_WIDTH), f32), pltpu.VMEM((seq, GLA_WIDTH), f32),
                        pltpu.VMEM((GLA_WIDTH, GLA_KW), f32), pltpu.VMEM((GLA_WIDTH, GLA_KW), f32),
                        pltpu.VMEM((c + s, GLA_KW), f32), pltpu.VMEM((c + s, GLA_KW), f32),
                        pltpu.VMEM((c + s, GLA_KW), f32), pltpu.VMEM((c + s, GLA_KW), f32),
                        pltpu.VMEM((s * c, GLA_KW), bf16), pltpu.VMEM((s * c, GLA_KW), bf16)],
        compiler_params=_cparams("parallel"),
        name="gla",
    )(q, k, v, r, lr, *params, *consts)


def _short_conv_lanes(u, w):
    n = u.shape[1]
    t = lax.broadcasted_iota(jnp.int32, u.shape, 1)
    prev = jnp.where(t == 0, 0.0, pltpu.roll(u, 1, axis=1))
    nxt = jnp.where(t == n - 1, 0.0, pltpu.roll(u, n - 1, axis=1))
    return prev * w[:, 0:1] + u * w[:, 1:2] + nxt * w[:, 2:3] + w[:, 3:4]


def _short_conv_rows(u, w):
    n = u.shape[0]
    t = lax.broadcasted_iota(jnp.int32, u.shape, 0)
    prev = jnp.where(t == 0, 0.0, pltpu.roll(u, 1, axis=0))
    nxt = jnp.where(t == n - 1, 0.0, pltpu.roll(u, n - 1, axis=0))
    return prev * w[0:1] + u * w[1:2] + nxt * w[2:3] + w[3:4]


def _hyena_kernel(v_ref, x1_ref, x2_ref, cwt_v_ref, cwt_x1_ref, cw_x2_ref, a_ref, b_ref, corr_ref,
                  wre_ref, wim_ref, out_ref, src_ref, acc1_ref, acc2_ref):
    order = pl.program_id(2)
    ft = pl.program_id(3)
    n_ft = pl.num_programs(3)
    nb, cb, seq = v_ref.shape
    rows = nb * cb

    def stacked(w):
        return jnp.concatenate([w] * nb, axis=0)

    @pl.when(jnp.logical_and(order == 0, ft == 0))
    def _():
        v = _short_conv_lanes(v_ref[...].reshape(rows, seq).astype(f32), stacked(cwt_v_ref[...]))
        src_ref[...] = v.astype(bf16)
        acc1_ref[...] = jnp.zeros_like(acc1_ref)

    @pl.when(jnp.logical_and(order == 1, ft == 0))
    def _():
        x1 = _short_conv_lanes(x1_ref[...].reshape(rows, seq).astype(f32), stacked(cwt_x1_ref[...]))
        src_ref[...] = (x1 * acc1_ref[...]).astype(bf16)
        acc2_ref[...] = jnp.zeros_like(acc2_ref)

    src = src_ref[...]
    wre = wre_ref[...]
    wim = wim_ref[...]
    u_re = _dot(src, wre)
    u_im = _dot(src, wim)
    a = stacked(a_ref[0])
    b = stacked(b_ref[0])
    y_re = u_re * a - u_im * b
    y_im = u_re * b + u_im * a
    y_re = y_re.astype(bf16)

    def inverse(y_im):
        y_im = y_im.astype(bf16)

        @pl.when(order == 0)
        def _():
            acc1_ref[...] += _dot_nt(y_re, wre) + _dot_nt(y_im, wim)

        @pl.when(order == 1)
        def _():
            acc2_ref[...] += _dot_nt(wre, y_re) + _dot_nt(wim, y_im)

    @pl.when(ft == 0)
    def _():
        fix = u_im[:, :LANES] * stacked(corr_ref[0])
        inverse(jnp.concatenate([y_im[:, :LANES] + fix, y_im[:, LANES:]], axis=1))

    @pl.when(ft != 0)
    def _():
        inverse(y_im)

    @pl.when(jnp.logical_and(order == 1, ft == n_ft - 1))
    def _():
        y = acc2_ref[...]
        for i in range(nb):
            x2 = _short_conv_rows(x2_ref[i].astype(f32), cw_x2_ref[...])
            out_ref[i] = (x2 * y[:, i * cb:(i + 1) * cb]).astype(out_ref.dtype)


def _hyena(vx1t, x2, cwt, cw_x2, fa, fb, fcorr, wf):
    bsz, two_w, seq = vx1t.shape
    width = two_w // 2
    cb, nb, ftile = HY_CBLK, HY_BBLK, HY_FTILE
    n_cb = width // cb
    n_ft = seq // ftile
    rows = nb * cb
    filt = lambda w: pl.BlockSpec((1, cb, w), lambda c, b, o, f: (o, c, f))
    return pl.pallas_call(
        _hyena_kernel,
        grid=(n_cb, bsz // nb, HY_ORDER, n_ft),
        in_specs=[pl.BlockSpec((nb, cb, seq), lambda c, b, o, f: (b, c, 0)),
                  pl.BlockSpec((nb, cb, seq), lambda c, b, o, f: (b, n_cb + c, 0)),
                  pl.BlockSpec((nb, seq, cb), lambda c, b, o, f: (b, 0, c)),
                  pl.BlockSpec((cb, 4), lambda c, b, o, f: (c, 0)),
                  pl.BlockSpec((cb, 4), lambda c, b, o, f: (n_cb + c, 0)),
                  pl.BlockSpec((4, cb), lambda c, b, o, f: (0, c)),
                  filt(ftile), filt(ftile),
                  pl.BlockSpec((1, cb, LANES), lambda c, b, o, f: (o, c, 0)),
                  pl.BlockSpec((seq, ftile), lambda c, b, o, f: (0, f)),
                  pl.BlockSpec((seq, ftile), lambda c, b, o, f: (0, n_ft + f))],
        out_specs=pl.BlockSpec((nb, seq, cb), lambda c, b, o, f: (b, 0, c)),
        out_shape=jax.ShapeDtypeStruct((bsz, seq, width), bf16),
        scratch_shapes=[pltpu.VMEM((rows, seq), bf16), pltpu.VMEM((rows, seq), f32),
                        pltpu.VMEM((seq, rows), f32)],
        compiler_params=_cparams("parallel", "parallel", "arbitrary", "arbitrary"),
        name="hyena",
    )(vx1t, vx1t, x2, cwt, cwt, cw_x2, fa, fb, fcorr, wf, wf)


def _out_ffn_kernel(x_ref, og_ref, hy_ref, hnw_ref, wo_g_ref, wo_h_ref, fnw_ref, wg_ref, wu_ref, wd_ref,
                    *rest, final_norm):
    if final_norm:
        lnw_ref, out_ref = rest
    else:
        (out_ref,) = rest
    hy = (_rms(hy_ref[...].astype(f32)) * hnw_ref[...]).astype(bf16)
    x = x_ref[...] + _dot(og_ref[...], wo_g_ref[...]) + _dot(hy, wo_h_ref[...])
    h = (_rms(x) * fnw_ref[...]).astype(bf16)
    ff = (_silu(_dot(h, wg_ref[...])) * _dot(h, wu_ref[...])).astype(bf16)
    x = x + _dot(ff, wd_ref[...])
    if final_norm:
        x = _rms(x) * lnw_ref[...]
    out_ref[...] = x


def _out_ffn(x, og, hy, hnw, wo_g, wo_h, fnw, wg, wu, wd, lnw, *, tm):
    n, d = x.shape
    row = lambda w: pl.BlockSpec((tm, w), lambda i: (i, 0))
    weights = [hnw, wo_g, wo_h, fnw, wg, wu, wd] + ([] if lnw is None else [lnw])
    return pl.pallas_call(
        functools.partial(_out_ffn_kernel, final_norm=lnw is not None),
        grid=(n // tm,),
        in_specs=[row(d), row(og.shape[1]), row(hy.shape[1])] + [_const_spec(w.shape) for w in weights],
        out_specs=row(d),
        out_shape=jax.ShapeDtypeStruct((n, d), f32),
        compiler_params=_cparams("parallel"),
        name="out_ffn",
    )(x, og, hy, *weights)


def _prepare_layer(norm_mix, w_in, wg_f, bg_f, wg_b, bg_b, gla_norm, conv_w, conv_b, hy_norm, w_out,
                   norm_ffn, w_gate, w_up, w_down):
    d = w_in.shape[0]
    hw = hy_norm.shape[0]
    o_lr = 2 * GLA_KW + 2 * GLA_WIDTH
    o_hy = o_lr + 2 * GLA_GATE_RANK
    wq = w_in[:, :GLA_KW] * (GLA_DK ** -0.5)
    wa = jnp.concatenate([wq, w_in[:, GLA_KW:o_lr], w_in[:, o_hy + 2 * hw:]], axis=1).astype(bf16)
    wlr = jnp.zeros((d, LANES), f32).at[:, :2 * GLA_GATE_RANK].set(w_in[:, o_lr:o_hy]).astype(bf16)
    wt = w_in[:, o_hy:o_hy + 2 * hw].T.astype(bf16)
    wg = jnp.zeros((LANES, 2 * GLA_KW), f32)
    wg = wg.at[:GLA_GATE_RANK, :GLA_KW].set(wg_f).at[GLA_GATE_RANK:2 * GLA_GATE_RANK, GLA_KW:].set(wg_b)
    wgh, wgl = _split_bf16(wg)
    bg = jnp.concatenate([bg_f, bg_b])[None, :]
    cw = jnp.concatenate([conv_w, conv_b[None, :]], axis=0)
    return dict(
        nw=norm_mix[None, :], wa=wa, wlr=wlr, wt=wt, wgh=wgh, wgl=wgl, bg=bg, gnw=gla_norm[None, :],
        cwt=cw[:, :2 * hw].T, cw_x2=cw[:, 2 * hw:], hnw=hy_norm[None, :],
        wo_g=w_out[:GLA_WIDTH].astype(bf16), wo_h=w_out[GLA_WIDTH:].astype(bf16),
        fnw=norm_ffn[None, :], wg=w_gate.astype(bf16), wu=w_up.astype(bf16), wd=w_down.astype(bf16))


def _layer(x, p, filt, wf, lnw):
    bsz, seq, d = x.shape
    q, k, v, r, x2, lr, vx1t = _in_proj(x, p["nw"], p["wa"], p["wlr"], p["wt"], tm=512)
    og = _gla(q, k, v, r, lr, p["wgh"], p["wgl"], p["bg"], p["gnw"])
    hy = _hyena(vx1t, x2, p["cwt"], p["cw_x2"], *filt, wf)
    n = bsz * seq
    y = _out_ffn(x.reshape(n, d), og.reshape(n, -1), hy.reshape(n, -1), p["hnw"], p["wo_g"], p["wo_h"],
                 p["fnw"], p["wg"], p["wu"], p["wd"], lnw, tm=512)
    return y.reshape(bsz, seq, d)


def kernel(x_prompt, x_sample, norm_mix, w_in, gla_wg_f, gla_bg_f, gla_wg_b, gla_bg_b, gla_norm, hy_conv_w, hy_conv_b, hy_w1, hy_b1, hy_freq1, hy_w2, hy_b2, hy_freq2, hy_w3, hy_skip, hy_norm, w_out, norm_ffn, w_gate, w_up, w_down, norm_final):
    depth = w_in.shape[0]
    seq = x_prompt.shape[1]
    assert x_sample.shape[1] == seq
    width = hy_norm.shape[1]
    wf = _dft_matrix(seq)
    feats = _positional_features(seq)
    layers, filters = [], []
    for l in range(depth):
        layers.append(_prepare_layer(norm_mix[l], w_in[l], gla_wg_f[l], gla_bg_f[l], gla_wg_b[l], gla_bg_b[l],
                                     gla_norm[l], hy_conv_w[l], hy_conv_b[l], hy_norm[l], w_out[l],
                                     norm_ffn[l], w_gate[l], w_up[l], w_down[l]))
        filters.append(_hyena_filters(feats, hy_w1[l], hy_b1[l], hy_freq1[l], hy_w2[l], hy_b2[l], hy_freq2[l],
                                      hy_w3[l], hy_skip[l], wf, seq=seq, width=width))
    outs = []
    for x in (x_prompt, x_sample):
        for l in range(depth):
            x = _layer(x, layers[l], filters[l], wf, norm_final[None, :] if l == depth - 1 else None)
        outs.append(x)
    return tuple(outs)
```

```python
import functools
import math

import numpy as np
import jax
import jax.numpy as jnp
from jax import lax
from jax.experimental import pallas as pl
from jax.experimental.pallas import tpu as pltpu

f32 = jnp.float32
bf16 = jnp.bfloat16

LANES = 128
SUBLANES = 8
VMEM_LIMIT_BYTES = 56 * 1024 * 1024

NORM_EPS = 1e-6

GLA_HEADS = 4
GLA_DK = 64
GLA_DV = 128
GLA_KW = GLA_HEADS * GLA_DK
GLA_WIDTH = GLA_HEADS * GLA_DV
GLA_GATE_RANK = 16
GLA_GATE_TEMP = 16.0
GLA_CHUNK = 64
GLA_LEVELS = (64, 32, 16)
GLA_SUB = SUBLANES
GLA_UNROLL = 2

HY_ORDER = 2
HY_DIRS = 2
HY_BANDS = 8
HY_EMB = 1 + 2 * HY_BANDS
HY_FFN = 64
HY_FAST_DECAY = 0.3
HY_SLOW_DECAY = 1.5
HY_DECAY_TARGET = 1e-2
HY_FTILE = 512
HY_CBLK = 256
HY_BBLK = 2


def _cparams(*sem):
    return pltpu.CompilerParams(dimension_semantics=sem, vmem_limit_bytes=VMEM_LIMIT_BYTES)


def _const_spec(shape):
    nd = len(shape)
    return pl.BlockSpec(shape, lambda *_: (0,) * nd, pipeline_mode=pl.Buffered(1))


def _split_bf16(a):
    hi = a.astype(bf16)
    lo = (a - hi.astype(f32)).astype(bf16)
    return hi, lo


def _dot(a, b):
    return jnp.dot(a, b, preferred_element_type=f32)


def _dot_nt(a, b):
    return lax.dot_general(a, b, (((1,), (1,)), ((), ())), preferred_element_type=f32)


def _dot_tn(a, b):
    return lax.dot_general(a, b, (((0,), (0,)), ((), ())), preferred_element_type=f32)


def _dot3(a_hi, a_lo, b):
    b_hi, b_lo = _split_bf16(b)
    return _dot(a_hi, b_hi) + (_dot(a_lo, b_hi) + _dot(a_hi, b_lo))


def _dot2(a, w):
    a_hi, a_lo = _split_bf16(a)
    return _dot(a_hi, w) + _dot(a_lo, w)


def _rms(x):
    return x * lax.rsqrt(jnp.mean(x * x, axis=-1, keepdims=True) + NORM_EPS)


def _silu(x):
    return x / (1.0 + jnp.exp(-x))


def _dft_kernel(o_ref, *, n_fft, rows):
    half = n_fft // 2
    i = pl.program_id(0)
    n = lax.broadcasted_iota(jnp.int32, (rows, half), 0) + i * rows
    f = lax.broadcasted_iota(jnp.int32, (rows, half), 1)
    ang = ((n * f) & (n_fft - 1)).astype(f32) * (2.0 * math.pi / n_fft)
    o_ref[:, :half] = jnp.cos(ang).astype(o_ref.dtype)
    nyq = jnp.where((n & 1) == 0, 1.0, -1.0)
    im = jnp.where(f == 0, nyq, -jnp.sin(ang))
    o_ref[:, half:] = im.astype(o_ref.dtype)


def _dft_matrix(seq):
    rows = 256
    return pl.pallas_call(
        functools.partial(_dft_kernel, n_fft=2 * seq, rows=rows),
        grid=(seq // rows,),
        out_specs=pl.BlockSpec((rows, 2 * seq), lambda i: (i, 0)),
        out_shape=jax.ShapeDtypeStruct((seq, 2 * seq), bf16),
        compiler_params=_cparams("parallel"),
        name="dft_matrix",
    )()


def _filter_kernel(feats_ref, w1h_ref, w1l_ref, b1_ref, fr1_ref, w2h_ref, w2l_ref, b2_ref, fr2_ref,
                   w3fh_ref, w3fl_ref, w3bh_ref, w3bl_ref, delta_ref, skip_ref, wf_ref,
                   a_ref, b_ref, corr_ref, *, seq):
    n_fft = 2 * seq
    feats = feats_ref[...]
    hid = jnp.sin(fr1_ref[...] * (_dot3(w1h_ref[...], w1l_ref[...], feats) + b1_ref[...]))
    hid = jnp.sin(fr2_ref[...] * (_dot3(w2h_ref[...], w2l_ref[...], hid) + b2_ref[...]))
    fwd = _dot3(w3fh_ref[0], w3fl_ref[0], hid)
    bwd = _dot3(w3bh_ref[0], w3bl_ref[0], hid)
    lag = lax.broadcasted_iota(jnp.int32, fwd.shape, 1)
    t = lag.astype(f32) * (1.0 / (seq - 1))
    window = jnp.exp(-t * delta_ref[...])
    fwd = fwd * window
    bwd = jnp.where(lag == 0, 0.0, bwd * window)
    sym = fwd + bwd
    k_re = _dot2(sym, wf_ref[:, :seq])
    k_im = _dot2(fwd - bwd, wf_ref[:, seq:])
    k_ny = _dot2(sym, wf_ref[:, seq:seq + LANES])[:, 0:1]
    skip = skip_ref[0]
    fcol = lax.broadcasted_iota(jnp.int32, k_re.shape, 1)
    dc = fcol == 0
    a = (k_re + skip) * jnp.where(dc, 1.0 / n_fft, 2.0 / n_fft)
    a_ref[0] = a
    b_ref[0] = jnp.where(dc, 0.0, k_im * (2.0 / n_fft))
    lane = lax.broadcasted_iota(jnp.int32, (a.shape[0], LANES), 1)
    corr_ref[0] = jnp.where(lane == 0, (k_ny + skip) * (1.0 / n_fft) - a[:, 0:1], 0.0)


def _hyena_filters(feats, w1, b1, fr1, w2, b2, fr2, w3, skip, wf, *, seq, width):
    pad = LANES

    def pad2(m, r, c):
        return jnp.zeros((r, c), f32).at[:m.shape[0], :m.shape[1]].set(m)

    w1h, w1l = _split_bf16(pad2(w1.T, pad, pad))
    w2h, w2l = _split_bf16(pad2(w2.T, pad, pad))
    w3t = pad2(w3.T, w3.shape[1], pad).reshape(HY_ORDER, HY_DIRS, width, pad)
    w3fh, w3fl = _split_bf16(w3t[:, 0])
    w3bh, w3bl = _split_bf16(w3t[:, 1])
    col = lambda v: pad2(v[:, None], pad, 1)
    max_decay = math.log(HY_DECAY_TARGET) / HY_FAST_DECAY
    min_decay = math.log(HY_DECAY_TARGET) / HY_SLOW_DECAY
    delta = jnp.abs(jnp.linspace(min_decay, max_decay, width, dtype=f32))[:, None]
    cb = HY_CBLK
    full = lambda shape: pl.BlockSpec(shape, lambda o, c: (0,) * len(shape))
    w3spec = pl.BlockSpec((1, cb, pad), lambda o, c: (o, c, 0))
    out_spec = pl.BlockSpec((1, cb, seq), lambda o, c: (o, c, 0))
    return pl.pallas_call(
        functools.partial(_filter_kernel, seq=seq),
        grid=(HY_ORDER, width // cb),
        in_specs=[full((pad, seq)), full((pad, pad)), full((pad, pad)), full((pad, 1)), full((pad, 1)),
                  full((pad, pad)), full((pad, pad)), full((pad, 1)), full((pad, 1)),
                  w3spec, w3spec, w3spec, w3spec,
                  pl.BlockSpec((cb, 1), lambda o, c: (c, 0)),
                  pl.BlockSpec((1, cb, 1), lambda o, c: (o, c, 0)),
                  _const_spec((seq, 2 * seq))],
        out_specs=[out_spec, out_spec, pl.BlockSpec((1, cb, LANES), lambda o, c: (o, c, 0))],
        out_shape=[jax.ShapeDtypeStruct((HY_ORDER, width, seq), f32),
                   jax.ShapeDtypeStruct((HY_ORDER, width, seq), f32),
                   jax.ShapeDtypeStruct((HY_ORDER, width, LANES), f32)],
        compiler_params=_cparams("arbitrary", "arbitrary"),
        name="hyena_filters",
    )(feats, w1h, w1l, col(b1), col(fr1), w2h, w2l, col(b2), col(fr2),
      w3fh, w3fl, w3bh, w3bl, delta, skip[:, :, None], wf)


def _positional_features(seq):
    t = jnp.linspace(0.0, 1.0, seq, dtype=f32)[:, None]
    pos_w = (2.0 * math.pi / seq) * jnp.arange(seq, dtype=f32)[:, None]
    bands = jnp.linspace(1e-4, HY_BANDS - 1.0, HY_BANDS, dtype=f32)
    feats = jnp.concatenate([t, jnp.cos(bands * pos_w), -jnp.sin(bands * pos_w)], axis=-1)
    return jnp.zeros((LANES, seq), f32).at[:HY_EMB].set(feats.T)


def _in_proj_kernel(x_ref, nw_ref, wa_ref, wlr_ref, wt_ref,
                    q_ref, k_ref, v_ref, r_ref, x2_ref, lr_ref, vt_ref):
    h = (_rms(x_ref[0]) * nw_ref[...]).astype(bf16)
    pa = _dot(h, wa_ref[...])
    o = 0
    for ref in (q_ref, k_ref, v_ref, r_ref, x2_ref):
        w = ref.shape[-1]
        ref[0] = pa[:, o:o + w].astype(ref.dtype)
        o += w
    lr_ref[0] = _dot(h, wlr_ref[...])
    vt_ref[0] = _dot_nt(wt_ref[...], h).astype(vt_ref.dtype)


def _in_proj(x, nw, wa, wlr, wt, *, tm):
    bsz, seq, d = x.shape
    hw = wt.shape[0] // 2
    row = lambda w: pl.BlockSpec((1, tm, w), lambda b, t: (b, t, 0))
    return pl.pallas_call(
        _in_proj_kernel,
        grid=(bsz, seq // tm),
        in_specs=[row(d), _const_spec(nw.shape), _const_spec(wa.shape), _const_spec(wlr.shape),
                  _const_spec(wt.shape)],
        out_specs=[row(GLA_KW), row(GLA_KW), row(GLA_WIDTH), row(GLA_WIDTH), row(hw), row(LANES),
                   pl.BlockSpec((1, 2 * hw, tm), lambda b, t: (b, 0, t))],
        out_shape=[jax.ShapeDtypeStruct((bsz, seq, GLA_KW), bf16),
                   jax.ShapeDtypeStruct((bsz, seq, GLA_KW), bf16),
                   jax.ShapeDtypeStruct((bsz, seq, GLA_WIDTH), bf16),
                   jax.ShapeDtypeStruct((bsz, seq, GLA_WIDTH), bf16),
                   jax.ShapeDtypeStruct((bsz, seq, hw), bf16),
                   jax.ShapeDtypeStruct((bsz, seq, LANES), f32),
                   jax.ShapeDtypeStruct((bsz, 2 * hw, seq), bf16)],
        compiler_params=_cparams("parallel", "parallel"),
        name="in_proj",
    )(x, nw, wa, wlr, wt)


def _block_diag(n_blocks, nr, nc):
    rr = np.arange(n_blocks * nr)[:, None] // nr
    cc = np.arange(n_blocks * nc)[None, :] // nc
    return (rr == cc).astype(np.float32)


def _gla_constants(rev):
    c, s, h = GLA_CHUNK, GLA_SUB, GLA_HEADS
    i = np.arange(c)[:, None]
    j = np.arange(c)[None, :]
    tri = (j >= i) if rev else (j <= i)
    lane_j = np.arange(h * c)[None, :] % c
    query_rows = np.zeros((c, len(GLA_LEVELS)), np.float32)
    same = np.zeros((len(GLA_LEVELS), c, h * c), np.float32)
    for l, size in enumerate(GLA_LEVELS):
        upper = (i % size) >= size // 2
        query_rows[:, l:l + 1] = ~upper if rev else upper
        same[l] = (i // size) == (lane_j // size)
    lag = np.zeros((s, c, h * c), np.float32)
    for d in range(s):
        if rev:
            lag[d] = (lane_j == i + d) & (i % s + d <= s - 1)
        else:
            lag[d] = (lane_j == i - d) & (i % s >= d)
    return tri.astype(np.float32), query_rows, same[1:], lag


def _interleave(*stages):
    live = list(stages)
    while live:
        for gen in list(live):
            try:
                next(gen)
            except StopIteration:
                live.remove(gen)


def _gla_chunk(t0, rev, q_ref, k_ref, v_ref, g_ref, st_ref, o_ref, p_ref, sin_ref,
               tri_ref, rows_ref, same_ref, lag_ref, bd2_ref, bd4_ref, ones_bd_ref):
    c, s, nh = GLA_CHUNK, GLA_SUB, GLA_HEADS
    kw = GLA_KW
    q = q_ref[0, pl.ds(t0, c), :].astype(f32)
    k = k_ref[0, pl.ds(t0, c), :].astype(f32)
    v = v_ref[0, pl.ds(t0, c), :]
    g = g_ref[pl.ds(t0, c), :]
    g_hi, g_lo = _split_bf16(g)
    tri = tri_ref[...]
    b = _dot(tri, g_hi) + _dot(tri, g_lo)
    yield
    b_end = b[0:1] if rev else b[c - 1:c]

    def stack_heads(x):
        return jnp.concatenate([x] * nh, axis=0)

    st = st_ref[...]
    sin_ref[...] = st.astype(bf16)
    bd4 = bd4_ref[...]
    vst = jnp.concatenate([v[:, h * GLA_DV:(h + 1) * GLA_DV] for h in range(nh)], axis=0)
    kt_bd = stack_heads((k * jnp.exp(b_end - b)).astype(bf16)) * bd4
    st_ref[...] = st * jnp.exp(b_end) + _dot_tn(vst, kt_bd)
    yield

    a = None
    for l, size in enumerate(GLA_LEVELS):
        half = size // 2
        ref_rows = [blk * size + (half if rev else half - 1) for blk in range(c // size)]
        ref = jnp.concatenate([jnp.broadcast_to(b[r:r + 1, :], (size, kw)) for r in ref_rows], axis=0)
        is_q = rows_ref[:, l:l + 1] > 0.5
        e = jnp.exp(jnp.minimum(jnp.where(is_q, b - ref, ref - b), 0.0))
        zero = jnp.zeros_like(q)
        ql = jnp.where(is_q, q * e, zero).astype(bf16)
        kl = jnp.where(is_q, zero, k * e).astype(bf16)
        parts = []
        for p in range(kw // LANES):
            sl = slice(p * LANES, (p + 1) * LANES)
            kbd = jnp.concatenate([kl[:, sl]] * 2, axis=0) * bd2_ref[...]
            parts.append(_dot_nt(ql[:, sl], kbd))
        al = jnp.concatenate(parts, axis=1)
        if l > 0:
            al = al * same_ref[l - 1]
        a = al if a is None else a + al
        yield

    gam = jnp.exp(g)
    x = k
    for d in range(s):
        if d > 0:
            x3 = x.reshape(c // s, s, kw)
            x = pltpu.roll(x3, (s - 1) if rev else 1, axis=1).reshape(c, kw) * gam
        p_ref[d * c:(d + 1) * c, :] = (q * x).astype(bf16)
    red = _dot(p_ref[...], ones_bd_ref[...])
    yield
    for d in range(s):
        a = a + red[d * c:(d + 1) * c, :] * lag_ref[d]

    a_st = stack_heads(a.astype(bf16)) * bd4
    qs_st = stack_heads((q * jnp.exp(b)).astype(bf16)) * bd4
    o_st = _dot(a_st, vst) + _dot_nt(qs_st, sin_ref[...])
    yield
    for h in range(nh):
        o_ref[pl.ds(t0, c), h * GLA_DV:(h + 1) * GLA_DV] = o_st[h * c:(h + 1) * c, :]


def _gla_kernel(q_ref, k_ref, v_ref, r_ref, lr_ref, wgh_ref, wgl_ref, bg_ref, nw_ref,
                trif_ref, rowsf_ref, samef_ref, lagf_ref, trib_ref, rowsb_ref, sameb_ref, lagb_ref,
                bd2_ref, bd4_ref, ones_bd_ref,
                out_ref,
                gf_ref, gb_ref, of_ref, ob_ref, stf_ref, stb_ref, p_ref, sin_ref, *, seq, rows):
    c = GLA_CHUNK
    kw = GLA_KW

    def gate_rows(i, carry):
        r0 = pl.multiple_of(i * rows, rows)
        lr_hi, lr_lo = _split_bf16(lr_ref[0, pl.ds(r0, rows), :])
        wgh = wgh_ref[...]
        z = _dot(lr_hi, wgh) + (_dot(lr_lo, wgh) + _dot(lr_hi, wgl_ref[...])) + bg_ref[...]
        g = (jnp.minimum(z, 0.0) - jnp.log(1.0 + jnp.exp(-jnp.abs(z)))) * (1.0 / GLA_GATE_TEMP)
        gf_ref[pl.ds(r0, rows), :] = g[:, :kw]
        gb_ref[pl.ds(r0, rows), :] = g[:, kw:]
        return carry

    lax.fori_loop(0, seq // rows, gate_rows, 0)

    stf_ref[...] = jnp.zeros_like(stf_ref)
    stb_ref[...] = jnp.zeros_like(stb_ref)

    consts = (bd2_ref, bd4_ref, ones_bd_ref)
    n_chunks = seq // c

    def scan(i, carry):
        chains = []
        for u in range(GLA_UNROLL):
            n = i * GLA_UNROLL + u
            tf = pl.multiple_of(n * c, c)
            tb = pl.multiple_of((n_chunks - 1 - n) * c, c)
            chains.append(_gla_chunk(tf, False, q_ref, k_ref, v_ref, gf_ref, stf_ref, of_ref,
                                     p_ref.at[2 * u], sin_ref.at[2 * u],
                                     trif_ref, rowsf_ref, samef_ref, lagf_ref, *consts))
            chains.append(_gla_chunk(tb, True, q_ref, k_ref, v_ref, gb_ref, stb_ref, ob_ref,
                                     p_ref.at[2 * u + 1], sin_ref.at[2 * u + 1],
                                     trib_ref, rowsb_ref, sameb_ref, lagb_ref, *consts))
        _interleave(*chains)
        return carry

    lax.fori_loop(0, n_chunks // GLA_UNROLL, scan, 0)

    def finish(i, carry):
        r0 = pl.multiple_of(i * rows, rows)
        o = of_ref[pl.ds(r0, rows), :] + ob_ref[pl.ds(r0, rows), :]
        gate = _silu(r_ref[0, pl.ds(r0, rows), :].astype(f32))
        nw = nw_ref[...]
        for h in range(GLA_HEADS):
            sl = slice(h * GLA_DV, (h + 1) * GLA_DV)
            out_ref[0, pl.ds(r0, rows), sl] = (_rms(o[:, sl]) * nw * gate[:, sl]).astype(out_ref.dtype)
        return carry

    lax.fori_loop(0, seq // rows, finish, 0)


def _gla(q, k, v, r, lr, wgh, wgl, bg, nw):
    bsz, seq, _ = q.shape
    c, s, h = GLA_CHUNK, GLA_SUB, GLA_HEADS
    consts = []
    for rev in (False, True):
        tri, rows, same, lag = _gla_constants(rev)
        consts += [jnp.asarray(tri, bf16), jnp.asarray(rows), jnp.asarray(same), jnp.asarray(lag)]
    consts += [jnp.asarray(_block_diag(2, c, GLA_DK), bf16),
               jnp.asarray(_block_diag(h, c, GLA_DK), bf16),
               jnp.asarray(_block_diag(h, GLA_DK, c), bf16)]
    per_b = lambda w: pl.BlockSpec((1, seq, w), lambda b: (b, 0, 0))
    params = [wgh, wgl, bg, nw]
    return pl.pallas_call(
        functools.partial(_gla_kernel, seq=seq, rows=256),
        grid=(bsz,),
        in_specs=[per_b(GLA_KW), per_b(GLA_KW), per_b(GLA_WIDTH), per_b(GLA_WIDTH), per_b(LANES)]
        + [_const_spec(p.shape) for p in params] + [_const_spec(x.shape) for x in consts],
        out_specs=per_b(GLA_WIDTH),
        out_shape=jax.ShapeDtypeStruct((bsz, seq, GLA_WIDTH), bf16),
        scratch_shapes=[pltpu.VMEM((seq, GLA_KW), f32), pltpu.VMEM((seq, GLA_KW), f32),
                        pltpu.VMEM((seq, GLA_WIDTH), f32), pltpu.VMEM((seq, GLA_WIDTH), f32),
                        pltpu.VMEM((GLA_DV, GLA_KW), f32), pltpu.VMEM((GLA_DV, GLA_KW), f32),
                        pltpu.VMEM((2 * GLA_UNROLL, s * c, GLA_KW), bf16),
                        pltpu.VMEM((2 * GLA_UNROLL, GLA_DV, GLA_KW), bf16)],
        compiler_params=_cparams("parallel"),
        name="gla",
    )(q, k, v, r, lr, *params, *consts)


def _short_conv_lanes(u, w):
    n = u.shape[1]
    t = lax.broadcasted_iota(jnp.int32, u.shape, 1)
    prev = jnp.where(t == 0, 0.0, pltpu.roll(u, 1, axis=1))
    nxt = jnp.where(t == n - 1, 0.0, pltpu.roll(u, n - 1, axis=1))
    return prev * w[:, 0:1] + u * w[:, 1:2] + nxt * w[:, 2:3] + w[:, 3:4]


def _short_conv_rows(u, w):
    n = u.shape[0]
    t = lax.broadcasted_iota(jnp.int32, u.shape, 0)
    prev = jnp.where(t == 0, 0.0, pltpu.roll(u, 1, axis=0))
    nxt = jnp.where(t == n - 1, 0.0, pltpu.roll(u, n - 1, axis=0))
    return prev * w[0:1] + u * w[1:2] + nxt * w[2:3] + w[3:4]


def _hyena_kernel(v_ref, x1_ref, x2_ref, cwt_v_ref, cwt_x1_ref, cw_x2_ref, a_ref, b_ref, corr_ref,
                  wre_ref, wim_ref, out_ref, src_ref, acc1_ref, acc2_ref):
    order = pl.program_id(2)
    ft = pl.program_id(3)
    n_ft = pl.num_programs(3)
    nb, cb, seq = v_ref.shape
    rows = nb * cb

    def stacked(w):
        return jnp.concatenate([w] * nb, axis=0)

    @pl.when(jnp.logical_and(order == 0, ft == 0))
    def _():
        v = _short_conv_lanes(v_ref[...].reshape(rows, seq).astype(f32), stacked(cwt_v_ref[...]))
        src_ref[...] = v.astype(bf16)
        acc1_ref[...] = jnp.zeros_like(acc1_ref)

    @pl.when(jnp.logical_and(order == 1, ft == 0))
    def _():
        x1 = _short_conv_lanes(x1_ref[...].reshape(rows, seq).astype(f32), stacked(cwt_x1_ref[...]))
        src_ref[...] = (x1 * acc1_ref[...]).astype(bf16)
        acc2_ref[...] = jnp.zeros_like(acc2_ref)

    src = src_ref[...]
    wre = wre_ref[...]
    wim = wim_ref[...]
    u_re = _dot(src, wre)
    u_im = _dot(src, wim)
    a = stacked(a_ref[0])
    b = stacked(b_ref[0])
    y_re = u_re * a - u_im * b
    y_im = u_re * b + u_im * a
    y_re = y_re.astype(bf16)

    def inverse(y_im):
        y_im = y_im.astype(bf16)

        @pl.when(order == 0)
        def _():
            acc1_ref[...] += _dot_nt(y_re, wre) + _dot_nt(y_im, wim)

        @pl.when(order == 1)
        def _():
            acc2_ref[...] += _dot_nt(wre, y_re) + _dot_nt(wim, y_im)

    @pl.when(ft == 0)
    def _():
        fix = u_im[:, :LANES] * stacked(corr_ref[0])
        inverse(jnp.concatenate([y_im[:, :LANES] + fix, y_im[:, LANES:]], axis=1))

    @pl.when(ft != 0)
    def _():
        inverse(y_im)

    @pl.when(jnp.logical_and(order == 1, ft == n_ft - 1))
    def _():
        y = acc2_ref[...]
        for i in range(nb):
            x2 = _short_conv_rows(x2_ref[i].astype(f32), cw_x2_ref[...])
            out_ref[i] = (x2 * y[:, i * cb:(i + 1) * cb]).astype(out_ref.dtype)


def _hyena(vx1t, x2, cwt, cw_x2, fa, fb, fcorr, wf):
    bsz, two_w, seq = vx1t.shape
    width = two_w // 2
    cb, nb, ftile = HY_CBLK, HY_BBLK, HY_FTILE
    n_cb = width // cb
    n_ft = seq // ftile
    rows = nb * cb
    filt = lambda w: pl.BlockSpec((1, cb, w), lambda c, b, o, f: (o, c, f))
    return pl.pallas_call(
        _hyena_kernel,
        grid=(n_cb, bsz // nb, HY_ORDER, n_ft),
        in_specs=[pl.BlockSpec((nb, cb, seq), lambda c, b, o, f: (b, c, 0)),
                  pl.BlockSpec((nb, cb, seq), lambda c, b, o, f: (b, n_cb + c, 0)),
                  pl.BlockSpec((nb, seq, cb), lambda c, b, o, f: (b, 0, c)),
                  pl.BlockSpec((cb, 4), lambda c, b, o, f: (c, 0)),
                  pl.BlockSpec((cb, 4), lambda c, b, o, f: (n_cb + c, 0)),
                  pl.BlockSpec((4, cb), lambda c, b, o, f: (0, c)),
                  filt(ftile), filt(ftile),
                  pl.BlockSpec((1, cb, LANES), lambda c, b, o, f: (o, c, 0)),
                  pl.BlockSpec((seq, ftile), lambda c, b, o, f: (0, f)),
                  pl.BlockSpec((seq, ftile), lambda c, b, o, f: (0, n_ft + f))],
        out_specs=pl.BlockSpec((nb, seq, cb), lambda c, b, o, f: (b, 0, c)),
        out_shape=jax.ShapeDtypeStruct((bsz, seq, width), bf16),
        scratch_shapes=[pltpu.VMEM((rows, seq), bf16), pltpu.VMEM((rows, seq), f32),
                        pltpu.VMEM((seq, rows), f32)],
        compiler_params=_cparams("parallel", "parallel", "arbitrary", "arbitrary"),
        name="hyena",
    )(vx1t, vx1t, x2, cwt, cwt, cw_x2, fa, fb, fcorr, wf, wf)


def _out_ffn_kernel(x_ref, og_ref, hy_ref, hnw_ref, wo_g_ref, wo_h_ref, fnw_ref, wg_ref, wu_ref, wd_ref,
                    *rest, final_norm):
    if final_norm:
        lnw_ref, out_ref = rest
    else:
        (out_ref,) = rest
    hy = (_rms(hy_ref[...].astype(f32)) * hnw_ref[...]).astype(bf16)
    x = x_ref[...] + _dot(og_ref[...], wo_g_ref[...]) + _dot(hy, wo_h_ref[...])
    h = (_rms(x) * fnw_ref[...]).astype(bf16)
    ff = (_silu(_dot(h, wg_ref[...])) * _dot(h, wu_ref[...])).astype(bf16)
    x = x + _dot(ff, wd_ref[...])
    if final_norm:
        x = _rms(x) * lnw_ref[...]
    out_ref[...] = x


def _out_ffn(x, og, hy, hnw, wo_g, wo_h, fnw, wg, wu, wd, lnw, *, tm):
    n, d = x.shape
    row = lambda w: pl.BlockSpec((tm, w), lambda i: (i, 0))
    weights = [hnw, wo_g, wo_h, fnw, wg, wu, wd] + ([] if lnw is None else [lnw])
    return pl.pallas_call(
        functools.partial(_out_ffn_kernel, final_norm=lnw is not None),
        grid=(n // tm,),
        in_specs=[row(d), row(og.shape[1]), row(hy.shape[1])] + [_const_spec(w.shape) for w in weights],
        out_specs=row(d),
        out_shape=jax.ShapeDtypeStruct((n, d), f32),
        compiler_params=_cparams("parallel"),
        name="out_ffn",
    )(x, og, hy, *weights)


def _prepare_layer(norm_mix, w_in, wg_f, bg_f, wg_b, bg_b, gla_norm, conv_w, conv_b, hy_norm, w_out,
                   norm_ffn, w_gate, w_up, w_down):
    d = w_in.shape[0]
    hw = hy_norm.shape[0]
    o_lr = 2 * GLA_KW + 2 * GLA_WIDTH
    o_hy = o_lr + 2 * GLA_GATE_RANK
    wq = w_in[:, :GLA_KW] * (GLA_DK ** -0.5)
    wa = jnp.concatenate([wq, w_in[:, GLA_KW:o_lr], w_in[:, o_hy + 2 * hw:]], axis=1).astype(bf16)
    wlr = jnp.zeros((d, LANES), f32).at[:, :2 * GLA_GATE_RANK].set(w_in[:, o_lr:o_hy]).astype(bf16)
    wt = w_in[:, o_hy:o_hy + 2 * hw].T.astype(bf16)
    wg = jnp.zeros((LANES, 2 * GLA_KW), f32)
    wg = wg.at[:GLA_GATE_RANK, :GLA_KW].set(wg_f).at[GLA_GATE_RANK:2 * GLA_GATE_RANK, GLA_KW:].set(wg_b)
    wgh, wgl = _split_bf16(wg)
    bg = jnp.concatenate([bg_f, bg_b])[None, :]
    cw = jnp.concatenate([conv_w, conv_b[None, :]], axis=0)
    return dict(
        nw=norm_mix[None, :], wa=wa, wlr=wlr, wt=wt, wgh=wgh, wgl=wgl, bg=bg, gnw=gla_norm[None, :],
        cwt=cw[:, :2 * hw].T, cw_x2=cw[:, 2 * hw:], hnw=hy_norm[None, :],
        wo_g=w_out[:GLA_WIDTH].astype(bf16), wo_h=w_out[GLA_WIDTH:].astype(bf16),
        fnw=norm_ffn[None, :], wg=w_gate.astype(bf16), wu=w_up.astype(bf16), wd=w_down.astype(bf16))


def _layer(x, p, filt, wf, lnw):
    bsz, seq, d = x.shape
    q, k, v, r, x2, lr, vx1t = _in_proj(x, p["nw"], p["wa"], p["wlr"], p["wt"], tm=512)
    og = _gla(q, k, v, r, lr, p["wgh"], p["wgl"], p["bg"], p["gnw"])
    hy = _hyena(vx1t, x2, p["cwt"], p["cw_x2"], *filt, wf)
    n = bsz * seq
    y = _out_ffn(x.reshape(n, d), og.reshape(n, -1), hy.reshape(n, -1), p["hnw"], p["wo_g"], p["wo_h"],
                 p["fnw"], p["wg"], p["wu"], p["wd"], lnw, tm=512)
    return y.reshape(bsz, seq, d)


def kernel(x_prompt, x_sample, norm_mix, w_in, gla_wg_f, gla_bg_f, gla_wg_b, gla_bg_b, gla_norm, hy_conv_w, hy_conv_b, hy_w1, hy_b1, hy_freq1, hy_w2, hy_b2, hy_freq2, hy_w3, hy_skip, hy_norm, w_out, norm_ffn, w_gate, w_up, w_down, norm_final):
    depth = w_in.shape[0]
    seq = x_prompt.shape[1]
    assert x_sample.shape[1] == seq
    width = hy_norm.shape[1]
    wf = _dft_matrix(seq)
    feats = _positional_features(seq)
    layers, filters = [], []
    for l in range(depth):
        layers.append(_prepare_layer(norm_mix[l], w_in[l], gla_wg_f[l], gla_bg_f[l], gla_wg_b[l], gla_bg_b[l],
                                     gla_norm[l], hy_conv_w[l], hy_conv_b[l], hy_norm[l], w_out[l],
                                     norm_ffn[l], w_gate[l], w_up[l], w_down[l]))
        filters.append(_hyena_filters(feats, hy_w1[l], hy_b1[l], hy_freq1[l], hy_w2[l], hy_b2[l], hy_freq2[l],
                                      hy_w3[l], hy_skip[l], wf, seq=seq, width=width))
    outs = []
    for x in (x_prompt, x_sample):
        for l in range(depth):
            x = _layer(x, layers[l], filters[l], wf, norm_final[None, :] if l == depth - 1 else None)
        outs.append(x)
    return tuple(outs)
```

```python
import functools
import math

import numpy as np
import jax
import jax.numpy as jnp
from jax import lax
from jax.experimental import pallas as pl
from jax.experimental.pallas import tpu as pltpu

f32 = jnp.float32
bf16 = jnp.bfloat16

LANES = 128
SUBLANES = 8
VMEM_LIMIT_BYTES = 56 * 1024 * 1024

NORM_EPS = 1e-6

GLA_HEADS = 4
GLA_DK = 64
GLA_DV = 128
GLA_KW = GLA_HEADS * GLA_DK
GLA_WIDTH = GLA_HEADS * GLA_DV
GLA_GATE_RANK = 16
GLA_GATE_TEMP = 16.0
GLA_CHUNK = 64
GLA_LEVELS = (64, 32, 16)
GLA_SUB = SUBLANES
GLA_UNROLL = 2

HY_ORDER = 2
HY_DIRS = 2
HY_BANDS = 8
HY_EMB = 1 + 2 * HY_BANDS
HY_FFN = 64
HY_FAST_DECAY = 0.3
HY_SLOW_DECAY = 1.5
HY_DECAY_TARGET = 1e-2
HY_PHASES = 4
HY_FTILE = 256
HY_CBLK = 256
HY_BBLK = 2
HY_RB = 16


def _cparams(*sem):
    return pltpu.CompilerParams(dimension_semantics=sem, vmem_limit_bytes=VMEM_LIMIT_BYTES)


def _const_spec(shape):
    nd = len(shape)
    return pl.BlockSpec(shape, lambda *_: (0,) * nd, pipeline_mode=pl.Buffered(1))


def _split_bf16(a):
    hi = a.astype(bf16)
    lo = (a - hi.astype(f32)).astype(bf16)
    return hi, lo


def _dot(a, b):
    return jnp.dot(a, b, preferred_element_type=f32)


def _dot_nt(a, b):
    return lax.dot_general(a, b, (((1,), (1,)), ((), ())), preferred_element_type=f32)


def _dot_tn(a, b):
    return lax.dot_general(a, b, (((0,), (0,)), ((), ())), preferred_element_type=f32)


def _dot3(a_hi, a_lo, b):
    b_hi, b_lo = _split_bf16(b)
    return _dot(a_hi, b_hi) + (_dot(a_lo, b_hi) + _dot(a_hi, b_lo))


def _dot2(a, w):
    a_hi, a_lo = _split_bf16(a)
    return _dot(a_hi, w) + _dot(a_lo, w)


def _rms(x):
    return x * lax.rsqrt(jnp.mean(x * x, axis=-1, keepdims=True) + NORM_EPS)


def _silu(x):
    return x / (1.0 + jnp.exp(-x))


def _dft_kernel(o_ref, *, n_fft, rows):
    half = n_fft // 2
    i = pl.program_id(0)
    n = lax.broadcasted_iota(jnp.int32, (rows, half), 0) + i * rows
    f = lax.broadcasted_iota(jnp.int32, (rows, half), 1)
    ang = ((n * f) & (n_fft - 1)).astype(f32) * (2.0 * math.pi / n_fft)
    o_ref[:, :half] = jnp.cos(ang).astype(o_ref.dtype)
    nyq = jnp.where((n & 1) == 0, 1.0, -1.0)
    im = jnp.where(f == 0, nyq, -jnp.sin(ang))
    o_ref[:, half:] = im.astype(o_ref.dtype)


def _dft_matrix(seq):
    rows = 256
    return pl.pallas_call(
        functools.partial(_dft_kernel, n_fft=2 * seq, rows=rows),
        grid=(seq // rows,),
        out_specs=pl.BlockSpec((rows, 2 * seq), lambda i: (i, 0)),
        out_shape=jax.ShapeDtypeStruct((seq, 2 * seq), bf16),
        compiler_params=_cparams("parallel"),
        name="dft_matrix",
    )()


def _filter_kernel(feats_ref, lag_ref, w1h_ref, w1l_ref, b1_ref, fr1_ref, w2h_ref, w2l_ref, b2_ref, fr2_ref,
                   w3fh_ref, w3fl_ref, w3bh_ref, w3bl_ref, delta_ref, skip_ref, ws_ref,
                   a_ref, b_ref, corr_ref, *, seq):
    nph = HY_PHASES
    ls = seq // nph
    ns = 2 * ls
    feats = feats_ref[...]
    hid = jnp.sin(fr1_ref[...] * (_dot3(w1h_ref[...], w1l_ref[...], feats) + b1_ref[...]))
    hid = jnp.sin(fr2_ref[...] * (_dot3(w2h_ref[...], w2l_ref[...], hid) + b2_ref[...]))
    fwd = _dot3(w3fh_ref[0], w3fl_ref[0], hid)
    bwd = _dot3(w3bh_ref[0], w3bl_ref[0], hid)
    lag = lag_ref[...]
    window = jnp.exp(-(lag * (1.0 / (seq - 1))) * delta_ref[...])
    fwd = fwd * window
    bwd = jnp.where(lag == 0.0, 0.0, bwd * window)
    mu = lax.broadcasted_iota(jnp.int32, (fwd.shape[0], ls), 1)
    first = mu == 0

    def phase(x, p):
        return x[:, p * ls:(p + 1) * ls]

    def delayed(x):
        return jnp.where(first, 0.0, pltpu.roll(x, 1, axis=1))

    skip = skip_ref[0]
    lane = lax.broadcasted_iota(jnp.int32, (fwd.shape[0], LANES), 1)
    for d in range(-(nph - 1), nph):
        if d >= 0:
            pos = phase(fwd, d)
            neg = phase(bwd, 0) if d == 0 else delayed(phase(bwd, nph - d))
        else:
            pos = jnp.where(first, phase(bwd, -d), delayed(phase(fwd, nph + d)))
            neg = jnp.where(first, 0.0, phase(bwd, -d))
        sym = pos + neg
        g_re = _dot2(sym, ws_ref[:, :ls])
        g_im = _dot2(pos - neg, ws_ref[:, ls:])
        g_ny = _dot2(sym, ws_ref[:, ls:ls + LANES])[:, 0:1]
        if d == 0:
            g_re = g_re + skip
            g_ny = g_ny + skip
        a = g_re * jnp.where(first, 1.0 / ns, 2.0 / ns)
        a_ref[0, d + nph - 1] = a
        b_ref[0, d + nph - 1] = jnp.where(first, 0.0, g_im * (2.0 / ns))
        corr_ref[0, d + nph - 1] = jnp.where(lane == 0, g_ny * (1.0 / ns) - a[:, 0:1], 0.0)


def _hyena_filters(feats, lag, w1, b1, fr1, w2, b2, fr2, w3, skip, ws, *, seq, width):
    pad = LANES
    nd = 2 * HY_PHASES - 1
    ls = seq // HY_PHASES

    def pad2(m, r, c):
        return jnp.zeros((r, c), f32).at[:m.shape[0], :m.shape[1]].set(m)

    w1h, w1l = _split_bf16(pad2(w1.T, pad, pad))
    w2h, w2l = _split_bf16(pad2(w2.T, pad, pad))
    w3t = pad2(w3.T, w3.shape[1], pad).reshape(HY_ORDER, HY_DIRS, width, pad)
    w3fh, w3fl = _split_bf16(w3t[:, 0])
    w3bh, w3bl = _split_bf16(w3t[:, 1])
    col = lambda v: pad2(v[:, None], pad, 1)
    max_decay = math.log(HY_DECAY_TARGET) / HY_FAST_DECAY
    min_decay = math.log(HY_DECAY_TARGET) / HY_SLOW_DECAY
    delta = jnp.abs(jnp.linspace(min_decay, max_decay, width, dtype=f32))[:, None]
    cb = HY_CBLK
    full = lambda shape: pl.BlockSpec(shape, lambda o, c: (0,) * len(shape))
    w3spec = pl.BlockSpec((1, cb, pad), lambda o, c: (o, c, 0))
    out_spec = pl.BlockSpec((1, nd, cb, ls), lambda o, c: (o, 0, c, 0))
    return pl.pallas_call(
        functools.partial(_filter_kernel, seq=seq),
        grid=(HY_ORDER, width // cb),
        in_specs=[full((pad, seq)), full((1, seq)),
                  full((pad, pad)), full((pad, pad)), full((pad, 1)), full((pad, 1)),
                  full((pad, pad)), full((pad, pad)), full((pad, 1)), full((pad, 1)),
                  w3spec, w3spec, w3spec, w3spec,
                  pl.BlockSpec((cb, 1), lambda o, c: (c, 0)),
                  pl.BlockSpec((1, cb, 1), lambda o, c: (o, c, 0)),
                  _const_spec(ws.shape)],
        out_specs=[out_spec, out_spec, pl.BlockSpec((1, nd, cb, LANES), lambda o, c: (o, 0, c, 0))],
        out_shape=[jax.ShapeDtypeStruct((HY_ORDER, nd, width, ls), f32),
                   jax.ShapeDtypeStruct((HY_ORDER, nd, width, ls), f32),
                   jax.ShapeDtypeStruct((HY_ORDER, nd, width, LANES), f32)],
        compiler_params=_cparams("arbitrary", "arbitrary"),
        name="hyena_filters",
    )(feats, lag, w1h, w1l, col(b1), col(fr1), w2h, w2l, col(b2), col(fr2),
      w3fh, w3fl, w3bh, w3bl, delta, skip[:, :, None], ws)


def _positional_features(seq):
    t = jnp.linspace(0.0, 1.0, seq, dtype=f32)[:, None]
    pos_w = (2.0 * math.pi / seq) * jnp.arange(seq, dtype=f32)[:, None]
    bands = jnp.linspace(1e-4, HY_BANDS - 1.0, HY_BANDS, dtype=f32)
    feats = jnp.concatenate([t, jnp.cos(bands * pos_w), -jnp.sin(bands * pos_w)], axis=-1)
    ls = seq // HY_PHASES
    col = np.arange(seq)
    lag = HY_PHASES * (col % ls) + col // ls
    feats_t = jnp.zeros((LANES, seq), f32).at[:HY_EMB].set(feats[lag].T)
    return feats_t, jnp.asarray(lag, f32)[None, :]


def _in_proj_kernel(x_ref, nw_ref, wa_ref, wlr_ref, wt_ref,
                    q_ref, k_ref, v_ref, r_ref, lr_ref, x2_ref, vt_ref):
    nph = HY_PHASES
    tmr = x_ref.shape[1]
    d = x_ref.shape[2] // nph
    h = jnp.concatenate([_rms(x_ref[0, :, r * d:(r + 1) * d]) * nw_ref[...] for r in range(nph)], axis=0)
    h = h.astype(bf16)
    pa = _dot(h, wa_ref[...])
    lr = _dot(h, wlr_ref[...])
    o = 0
    for ref in (q_ref, k_ref, v_ref, r_ref):
        w = ref.shape[-1] // nph
        for r in range(nph):
            ref[0, :, r * w:(r + 1) * w] = pa[r * tmr:(r + 1) * tmr, o:o + w].astype(ref.dtype)
        o += w
    for r in range(nph):
        lr_ref[0, :, r * LANES:(r + 1) * LANES] = lr[r * tmr:(r + 1) * tmr, :]
        x2_ref[0, r] = pa[r * tmr:(r + 1) * tmr, o:].astype(x2_ref.dtype)
    vt = _dot_nt(wt_ref[...], h).astype(vt_ref.dtype)
    for r in range(nph):
        vt_ref[0, r] = vt[:, r * tmr:(r + 1) * tmr]


def _in_proj(x, nw, wa, wlr, wt, *, tm):
    bsz, seq, d = x.shape
    nph = HY_PHASES
    hw = wt.shape[0] // 2
    sr, tmr = seq // nph, tm // nph
    row = lambda w: pl.BlockSpec((1, tmr, nph * w), lambda b, t: (b, t, 0))
    view = lambda w, dt: jax.ShapeDtypeStruct((bsz, sr, nph * w), dt)
    outs = pl.pallas_call(
        _in_proj_kernel,
        grid=(bsz, seq // tm),
        in_specs=[row(d), _const_spec(nw.shape), _const_spec(wa.shape), _const_spec(wlr.shape),
                  _const_spec(wt.shape)],
        out_specs=[row(GLA_KW), row(GLA_KW), row(GLA_WIDTH), row(GLA_WIDTH), row(LANES),
                   pl.BlockSpec((1, nph, tmr, hw), lambda b, t: (b, 0, t, 0)),
                   pl.BlockSpec((1, nph, 2 * hw, tmr), lambda b, t: (b, 0, 0, t))],
        out_shape=[view(GLA_KW, bf16), view(GLA_KW, bf16), view(GLA_WIDTH, bf16), view(GLA_WIDTH, bf16),
                   view(LANES, f32),
                   jax.ShapeDtypeStruct((bsz, nph, sr, hw), bf16),
                   jax.ShapeDtypeStruct((bsz, nph, 2 * hw, sr), bf16)],
        compiler_params=_cparams("parallel", "parallel"),
        name="in_proj",
    )(x.reshape(bsz, sr, nph * d), nw, wa, wlr, wt)
    token_major = [o.reshape(bsz, seq, -1) for o in outs[:5]]
    return (*token_major, outs[5], outs[6])


def _block_diag(n_blocks, nr, nc):
    rr = np.arange(n_blocks * nr)[:, None] // nr
    cc = np.arange(n_blocks * nc)[None, :] // nc
    return (rr == cc).astype(np.float32)


def _gla_constants(rev):
    c, s, h = GLA_CHUNK, GLA_SUB, GLA_HEADS
    i = np.arange(c)[:, None]
    j = np.arange(c)[None, :]
    tri = (j >= i) if rev else (j <= i)
    lane_j = np.arange(h * c)[None, :] % c
    query_rows = np.zeros((c, len(GLA_LEVELS)), np.float32)
    same = np.zeros((len(GLA_LEVELS), c, h * c), np.float32)
    for l, size in enumerate(GLA_LEVELS):
        upper = (i % size) >= size // 2
        query_rows[:, l:l + 1] = ~upper if rev else upper
        same[l] = (i // size) == (lane_j // size)
    lag = np.zeros((s, c, h * c), np.float32)
    for d in range(s):
        if rev:
            lag[d] = (lane_j == i + d) & (i % s + d <= s - 1)
        else:
            lag[d] = (lane_j == i - d) & (i % s >= d)
    return tri.astype(np.float32), query_rows, same[1:], lag


def _interleave(*stages):
    live = list(stages)
    while live:
        for gen in list(live):
            try:
                next(gen)
            except StopIteration:
                live.remove(gen)


def _gla_chunk(t0, rev, q_ref, k_ref, v_ref, g_ref, st_ref, o_ref, p_ref, sin_ref,
               tri_ref, rows_ref, same_ref, lag_ref, bd2_ref, bd4_ref, ones_bd_ref):
    c, s, nh = GLA_CHUNK, GLA_SUB, GLA_HEADS
    kw = GLA_KW
    q = q_ref[0, pl.ds(t0, c), :].astype(f32)
    k = k_ref[0, pl.ds(t0, c), :].astype(f32)
    v = v_ref[0, pl.ds(t0, c), :]
    g = g_ref[pl.ds(t0, c), :]
    g_hi, g_lo = _split_bf16(g)
    tri = tri_ref[...]
    b = _dot(tri, g_hi) + _dot(tri, g_lo)
    yield
    b_end = b[0:1] if rev else b[c - 1:c]

    def stack_heads(x):
        return jnp.concatenate([x] * nh, axis=0)

    st = st_ref[...]
    sin_ref[...] = st.astype(bf16)
    bd4 = bd4_ref[...]
    vst = jnp.concatenate([v[:, h * GLA_DV:(h + 1) * GLA_DV] for h in range(nh)], axis=0)
    kt_bd = stack_heads((k * jnp.exp(b_end - b)).astype(bf16)) * bd4
    st_ref[...] = st * jnp.exp(b_end) + _dot_tn(vst, kt_bd)
    yield

    a = None
    for l, size in enumerate(GLA_LEVELS):
        half = size // 2
        ref_rows = [blk * size + (half if rev else half - 1) for blk in range(c // size)]
        ref = jnp.concatenate([jnp.broadcast_to(b[r:r + 1, :], (size, kw)) for r in ref_rows], axis=0)
        is_q = rows_ref[:, l:l + 1] > 0.5
        e = jnp.exp(jnp.minimum(jnp.where(is_q, b - ref, ref - b), 0.0))
        zero = jnp.zeros_like(q)
        ql = jnp.where(is_q, q * e, zero).astype(bf16)
        kl = jnp.where(is_q, zero, k * e).astype(bf16)
        parts = []
        for p in range(kw // LANES):
            sl = slice(p * LANES, (p + 1) * LANES)
            kbd = jnp.concatenate([kl[:, sl]] * 2, axis=0) * bd2_ref[...]
            parts.append(_dot_nt(ql[:, sl], kbd))
        al = jnp.concatenate(parts, axis=1)
        if l > 0:
            al = al * same_ref[l - 1]
        a = al if a is None else a + al
        yield

    gam = jnp.exp(g)
    x = k
    for d in range(s):
        if d > 0:
            x3 = x.reshape(c // s, s, kw)
            x = pltpu.roll(x3, (s - 1) if rev else 1, axis=1).reshape(c, kw) * gam
        p_ref[d * c:(d + 1) * c, :] = (q * x).astype(bf16)
    red = _dot(p_ref[...], ones_bd_ref[...])
    yield
    for d in range(s):
        a = a + red[d * c:(d + 1) * c, :] * lag_ref[d]

    a_st = stack_heads(a.astype(bf16)) * bd4
    qs_st = stack_heads((q * jnp.exp(b)).astype(bf16)) * bd4
    o_st = _dot(a_st, vst) + _dot_nt(qs_st, sin_ref[...])
    yield
    for h in range(nh):
        o_ref[pl.ds(t0, c), h * GLA_DV:(h + 1) * GLA_DV] = o_st[h * c:(h + 1) * c, :]


def _gla_kernel(q_ref, k_ref, v_ref, r_ref, lr_ref, wgh_ref, wgl_ref, bg_ref, nw_ref,
                trif_ref, rowsf_ref, samef_ref, lagf_ref, trib_ref, rowsb_ref, sameb_ref, lagb_ref,
                bd2_ref, bd4_ref, ones_bd_ref,
                out_ref,
                gf_ref, gb_ref, of_ref, ob_ref, stf_ref, stb_ref, p_ref, sin_ref, *, seq, rows):
    c = GLA_CHUNK
    kw = GLA_KW

    def gate_rows(i, carry):
        r0 = pl.multiple_of(i * rows, rows)
        lr_hi, lr_lo = _split_bf16(lr_ref[0, pl.ds(r0, rows), :])
        wgh = wgh_ref[...]
        z = _dot(lr_hi, wgh) + (_dot(lr_lo, wgh) + _dot(lr_hi, wgl_ref[...])) + bg_ref[...]
        g = (jnp.minimum(z, 0.0) - jnp.log(1.0 + jnp.exp(-jnp.abs(z)))) * (1.0 / GLA_GATE_TEMP)
        gf_ref[pl.ds(r0, rows), :] = g[:, :kw]
        gb_ref[pl.ds(r0, rows), :] = g[:, kw:]
        return carry

    lax.fori_loop(0, seq // rows, gate_rows, 0)

    stf_ref[...] = jnp.zeros_like(stf_ref)
    stb_ref[...] = jnp.zeros_like(stb_ref)

    consts = (bd2_ref, bd4_ref, ones_bd_ref)
    n_chunks = seq // c

    def scan(i, carry):
        chains = []
        for u in range(GLA_UNROLL):
            n = i * GLA_UNROLL + u
            tf = pl.multiple_of(n * c, c)
            tb = pl.multiple_of((n_chunks - 1 - n) * c, c)
            chains.append(_gla_chunk(tf, False, q_ref, k_ref, v_ref, gf_ref, stf_ref, of_ref,
                                     p_ref.at[2 * u], sin_ref.at[2 * u],
                                     trif_ref, rowsf_ref, samef_ref, lagf_ref, *consts))
            chains.append(_gla_chunk(tb, True, q_ref, k_ref, v_ref, gb_ref, stb_ref, ob_ref,
                                     p_ref.at[2 * u + 1], sin_ref.at[2 * u + 1],
                                     trib_ref, rowsb_ref, sameb_ref, lagb_ref, *consts))
        _interleave(*chains)
        return carry

    lax.fori_loop(0, n_chunks // GLA_UNROLL, scan, 0)

    def finish(i, carry):
        r0 = pl.multiple_of(i * rows, rows)
        o = of_ref[pl.ds(r0, rows), :] + ob_ref[pl.ds(r0, rows), :]
        gate = _silu(r_ref[0, pl.ds(r0, rows), :].astype(f32))
        nw = nw_ref[...]
        for h in range(GLA_HEADS):
            sl = slice(h * GLA_DV, (h + 1) * GLA_DV)
            out_ref[0, pl.ds(r0, rows), sl] = (_rms(o[:, sl]) * nw * gate[:, sl]).astype(out_ref.dtype)
        return carry

    lax.fori_loop(0, seq // rows, finish, 0)


def _gla(q, k, v, r, lr, wgh, wgl, bg, nw):
    bsz, seq, _ = q.shape
    c, s, h = GLA_CHUNK, GLA_SUB, GLA_HEADS
    consts = []
    for rev in (False, True):
        tri, rows, same, lag = _gla_constants(rev)
        consts += [jnp.asarray(tri, bf16), jnp.asarray(rows), jnp.asarray(same), jnp.asarray(lag)]
    consts += [jnp.asarray(_block_diag(2, c, GLA_DK), bf16),
               jnp.asarray(_block_diag(h, c, GLA_DK), bf16),
               jnp.asarray(_block_diag(h, GLA_DK, c), bf16)]
    per_b = lambda w: pl.BlockSpec((1, seq, w), lambda b: (b, 0, 0))
    params = [wgh, wgl, bg, nw]
    return pl.pallas_call(
        functools.partial(_gla_kernel, seq=seq, rows=256),
        grid=(bsz,),
        in_specs=[per_b(GLA_KW), per_b(GLA_KW), per_b(GLA_WIDTH), per_b(GLA_WIDTH), per_b(LANES)]
        + [_const_spec(p.shape) for p in params] + [_const_spec(x.shape) for x in consts],
        out_specs=per_b(GLA_WIDTH),
        out_shape=jax.ShapeDtypeStruct((bsz, seq, GLA_WIDTH), bf16),
        scratch_shapes=[pltpu.VMEM((seq, GLA_KW), f32), pltpu.VMEM((seq, GLA_KW), f32),
                        pltpu.VMEM((seq, GLA_WIDTH), f32), pltpu.VMEM((seq, GLA_WIDTH), f32),
                        pltpu.VMEM((GLA_DV, GLA_KW), f32), pltpu.VMEM((GLA_DV, GLA_KW), f32),
                        pltpu.VMEM((2 * GLA_UNROLL, s * c, GLA_KW), bf16),
                        pltpu.VMEM((2 * GLA_UNROLL, GLA_DV, GLA_KW), bf16)],
        compiler_params=_cparams("parallel"),
        name="gla",
    )(q, k, v, r, lr, *params, *consts)


def _short_conv_phases(u, w, axis):
    nph = len(u)
    n = u[0].shape[axis]
    m = lax.broadcasted_iota(jnp.int32, u[0].shape, axis)
    before_first = jnp.where(m == 0, 0.0, pltpu.roll(u[nph - 1], 1, axis=axis))
    after_last = jnp.where(m == n - 1, 0.0, pltpu.roll(u[0], n - 1, axis=axis))
    out = []
    for r in range(nph):
        prev = u[r - 1] if r > 0 else before_first
        nxt = u[r + 1] if r < nph - 1 else after_last
        out.append(prev * w[0] + u[r] * w[1] + nxt * w[2] + w[3])
    return out


def _hyena_kernel(v_ref, x1_ref, x2_ref, cwt_v_ref, cwt_x1_ref, cw_x2_ref, a_ref, b_ref, corr_ref,
                  wre_ref, wim_ref, out_ref, src_ref, acc1_ref, acc2_ref, u_ref, y_ref):
    order = pl.program_id(2)
    ft = pl.program_id(3)
    n_ft = pl.num_programs(3)
    nb, nph, cb, ls = v_ref.shape
    rows = nb * cb

    rb = HY_RB

    def conv_to_src(u_ref_in, w_ref, gate_ref):
        def body(i, carry):
            bi = i // (cb // rb)
            crow = pl.multiple_of((i % (cb // rb)) * rb, rb)
            row0 = pl.multiple_of(i * rb, rb)
            w = w_ref[pl.ds(crow, rb), :]
            u = [u_ref_in[bi, r, pl.ds(crow, rb), :].astype(f32) for r in range(nph)]
            c = _short_conv_phases(u, [w[:, j:j + 1] for j in range(4)], axis=1)
            for r in range(nph):
                val = c[r] if gate_ref is None else c[r] * gate_ref[r, pl.ds(row0, rb), :]
                src_ref[r, pl.ds(row0, rb), :] = val.astype(bf16)
            return carry
        lax.fori_loop(0, rows // rb, body, 0, unroll=4)

    @pl.when(jnp.logical_and(order == 0, ft == 0))
    def _():
        conv_to_src(v_ref, cwt_v_ref, None)
        acc1_ref[...] = jnp.zeros_like(acc1_ref)

    @pl.when(jnp.logical_and(order == 1, ft == 0))
    def _():
        conv_to_src(x1_ref, cwt_x1_ref, acc1_ref)
        acc2_ref[...] = jnp.zeros_like(acc2_ref)

    wre = wre_ref[...]
    wim = wim_ref[...]
    ftile = wre.shape[1]
    for r in range(nph):
        u_ref[2 * r] = _dot(src_ref[r], wre)
        u_ref[2 * r + 1] = _dot(src_ref[r], wim)
    first_tile = (ft == 0).astype(f32)

    rb = HY_RB

    def combine(i, carry):
        row0 = pl.multiple_of(i * rb, rb)
        crow = pl.multiple_of((i % (cb // rb)) * rb, rb)
        for lane0 in range(0, ftile, LANES):
            cols = slice(lane0, lane0 + LANES)
            u_re = [u_ref[2 * r, pl.ds(row0, rb), cols] for r in range(nph)]
            u_im = [u_ref[2 * r + 1, pl.ds(row0, rb), cols] for r in range(nph)]
            for rp in range(nph):
                y_re = y_im = None
                for r in range(nph):
                    d = rp - r + nph - 1
                    a = a_ref[0, d, pl.ds(crow, rb), cols]
                    b = b_ref[0, d, pl.ds(crow, rb), cols]
                    t_re = u_re[r] * a - u_im[r] * b
                    t_im = u_re[r] * b + u_im[r] * a
                    if lane0 == 0:
                        t_im = t_im + u_im[r] * (corr_ref[0, d, pl.ds(crow, rb), :] * first_tile)
                    y_re = t_re if y_re is None else y_re + t_re
                    y_im = t_im if y_im is None else y_im + t_im
                y_ref[2 * rp, pl.ds(row0, rb), cols] = y_re.astype(bf16)
                y_ref[2 * rp + 1, pl.ds(row0, rb), cols] = y_im.astype(bf16)
        return carry

    lax.fori_loop(0, rows // rb, combine, 0)

    @pl.when(order == 0)
    def _():
        for rp in range(nph):
            acc1_ref[rp] += _dot_nt(y_ref[2 * rp], wre) + _dot_nt(y_ref[2 * rp + 1], wim)

    @pl.when(order == 1)
    def _():
        for rp in range(nph):
            acc2_ref[rp] += _dot_nt(wre, y_ref[2 * rp]) + _dot_nt(wim, y_ref[2 * rp + 1])

    @pl.when(jnp.logical_and(order == 1, ft == n_ft - 1))
    def _():
        w = cw_x2_ref[...]
        taps = [w[i:i + 1] for i in range(4)]
        for i in range(nb):
            x2 = _short_conv_phases([x2_ref[i, r].astype(f32) for r in range(nph)], taps, axis=0)
            for r in range(nph):
                out_ref[i, r] = (x2[r] * acc2_ref[r][:, i * cb:(i + 1) * cb]).astype(out_ref.dtype)


def _hyena(vx1t, x2, cwt, cw_x2, fa, fb, fcorr, ws):
    bsz, nph, two_w, ls = vx1t.shape
    width = two_w // 2
    cb, nb, ftile = HY_CBLK, HY_BBLK, HY_FTILE
    n_cb = width // cb
    n_ft = ls // ftile
    nd = fa.shape[1]
    rows = nb * cb
    filt = lambda w: pl.BlockSpec((1, nd, cb, w), lambda c, b, o, f: (o, 0, c, f))
    return pl.pallas_call(
        _hyena_kernel,
        grid=(n_cb, bsz // nb, HY_ORDER, n_ft),
        in_specs=[pl.BlockSpec((nb, nph, cb, ls), lambda c, b, o, f: (b, 0, c, 0)),
                  pl.BlockSpec((nb, nph, cb, ls), lambda c, b, o, f: (b, 0, n_cb + c, 0)),
                  pl.BlockSpec((nb, nph, ls, cb), lambda c, b, o, f: (b, 0, 0, c)),
                  pl.BlockSpec((cb, 4), lambda c, b, o, f: (c, 0)),
                  pl.BlockSpec((cb, 4), lambda c, b, o, f: (n_cb + c, 0)),
                  pl.BlockSpec((4, cb), lambda c, b, o, f: (0, c)),
                  filt(ftile), filt(ftile),
                  pl.BlockSpec((1, nd, cb, LANES), lambda c, b, o, f: (o, 0, c, 0)),
                  pl.BlockSpec((ls, ftile), lambda c, b, o, f: (0, f)),
                  pl.BlockSpec((ls, ftile), lambda c, b, o, f: (0, n_ft + f))],
        out_specs=pl.BlockSpec((nb, nph, ls, cb), lambda c, b, o, f: (b, 0, 0, c)),
        out_shape=jax.ShapeDtypeStruct((bsz, nph, ls, width), bf16),
        scratch_shapes=[pltpu.VMEM((nph, rows, ls), bf16), pltpu.VMEM((nph, rows, ls), f32),
                        pltpu.VMEM((nph, ls, rows), f32),
                        pltpu.VMEM((2 * nph, rows, ftile), f32), pltpu.VMEM((2 * nph, rows, ftile), bf16)],
        compiler_params=_cparams("parallel", "parallel", "arbitrary", "arbitrary"),
        name="hyena",
    )(vx1t, vx1t, x2, cwt, cwt, cw_x2, fa, fb, fcorr, ws, ws)


def _out_ffn_kernel(x_ref, og_ref, hy_ref, hnw_ref, wo_g_ref, wo_h_ref, fnw_ref, wg_ref, wu_ref, wd_ref,
                    *rest, final_norm):
    if final_norm:
        lnw_ref, out_ref = rest
    else:
        (out_ref,) = rest
    nph, tmr = hy_ref.shape[1], hy_ref.shape[2]
    d = x_ref.shape[2] // nph
    gw = og_ref.shape[2] // nph
    x = jnp.concatenate([x_ref[0, :, r * d:(r + 1) * d] for r in range(nph)], axis=0)
    og = jnp.concatenate([og_ref[0, :, r * gw:(r + 1) * gw] for r in range(nph)], axis=0)
    hy = jnp.concatenate([hy_ref[0, r] for r in range(nph)], axis=0)
    hy = (_rms(hy.astype(f32)) * hnw_ref[...]).astype(bf16)
    x = x + _dot(og, wo_g_ref[...]) + _dot(hy, wo_h_ref[...])
    h = (_rms(x) * fnw_ref[...]).astype(bf16)
    ff = (_silu(_dot(h, wg_ref[...])) * _dot(h, wu_ref[...])).astype(bf16)
    x = x + _dot(ff, wd_ref[...])
    if final_norm:
        x = _rms(x) * lnw_ref[...]
    for r in range(nph):
        out_ref[0, :, r * d:(r + 1) * d] = x[r * tmr:(r + 1) * tmr, :]


def _out_ffn(x, og, hy, hnw, wo_g, wo_h, fnw, wg, wu, wd, lnw, *, tm):
    bsz, seq, d = x.shape
    nph, sr, hw = hy.shape[1:]
    gw = og.shape[2]
    tmr = tm // nph
    row = lambda w: pl.BlockSpec((1, tmr, nph * w), lambda b, t: (b, t, 0))
    weights = [hnw, wo_g, wo_h, fnw, wg, wu, wd] + ([] if lnw is None else [lnw])
    out = pl.pallas_call(
        functools.partial(_out_ffn_kernel, final_norm=lnw is not None),
        grid=(bsz, seq // tm),
        in_specs=[row(d), row(gw), pl.BlockSpec((1, nph, tmr, hw), lambda b, t: (b, 0, t, 0))]
        + [_const_spec(w.shape) for w in weights],
        out_specs=row(d),
        out_shape=jax.ShapeDtypeStruct((bsz, sr, nph * d), f32),
        compiler_params=_cparams("parallel", "parallel"),
        name="out_ffn",
    )(x.reshape(bsz, sr, nph * d), og.reshape(bsz, sr, nph * gw), hy, *weights)
    return out.reshape(bsz, seq, d)


def _prepare_layer(norm_mix, w_in, wg_f, bg_f, wg_b, bg_b, gla_norm, conv_w, conv_b, hy_norm, w_out,
                   norm_ffn, w_gate, w_up, w_down):
    d = w_in.shape[0]
    hw = hy_norm.shape[0]
    o_lr = 2 * GLA_KW + 2 * GLA_WIDTH
    o_hy = o_lr + 2 * GLA_GATE_RANK
    wq = w_in[:, :GLA_KW] * (GLA_DK ** -0.5)
    wa = jnp.concatenate([wq, w_in[:, GLA_KW:o_lr], w_in[:, o_hy + 2 * hw:]], axis=1).astype(bf16)
    wlr = jnp.zeros((d, LANES), f32).at[:, :2 * GLA_GATE_RANK].set(w_in[:, o_lr:o_hy]).astype(bf16)
    wt = w_in[:, o_hy:o_hy + 2 * hw].T.astype(bf16)
    wg = jnp.zeros((LANES, 2 * GLA_KW), f32)
    wg = wg.at[:GLA_GATE_RANK, :GLA_KW].set(wg_f).at[GLA_GATE_RANK:2 * GLA_GATE_RANK, GLA_KW:].set(wg_b)
    wgh, wgl = _split_bf16(wg)
    bg = jnp.concatenate([bg_f, bg_b])[None, :]
    cw = jnp.concatenate([conv_w, conv_b[None, :]], axis=0)
    return dict(
        nw=norm_mix[None, :], wa=wa, wlr=wlr, wt=wt, wgh=wgh, wgl=wgl, bg=bg, gnw=gla_norm[None, :],
        cwt=cw[:, :2 * hw].T, cw_x2=cw[:, 2 * hw:], hnw=hy_norm[None, :],
        wo_g=w_out[:GLA_WIDTH].astype(bf16), wo_h=w_out[GLA_WIDTH:].astype(bf16),
        fnw=norm_ffn[None, :], wg=w_gate.astype(bf16), wu=w_up.astype(bf16), wd=w_down.astype(bf16))


def _layer(x, p, filt, wf, lnw):
    q, k, v, r, lr, x2, vx1t = _in_proj(x, p["nw"], p["wa"], p["wlr"], p["wt"], tm=512)
    og = _gla(q, k, v, r, lr, p["wgh"], p["wgl"], p["bg"], p["gnw"])
    hy = _hyena(vx1t, x2, p["cwt"], p["cw_x2"], *filt, wf)
    return _out_ffn(x, og, hy, p["hnw"], p["wo_g"], p["wo_h"], p["fnw"], p["wg"], p["wu"], p["wd"], lnw, tm=512)


def kernel(x_prompt, x_sample, norm_mix, w_in, gla_wg_f, gla_bg_f, gla_wg_b, gla_bg_b, gla_norm, hy_conv_w, hy_conv_b, hy_w1, hy_b1, hy_freq1, hy_w2, hy_b2, hy_freq2, hy_w3, hy_skip, hy_norm, w_out, norm_ffn, w_gate, w_up, w_down, norm_final):
    depth = w_in.shape[0]
    seq = x_prompt.shape[1]
    assert x_sample.shape[1] == seq
    width = hy_norm.shape[1]
    wf = _dft_matrix(seq // HY_PHASES)
    feats, lag = _positional_features(seq)
    layers, filters = [], []
    for l in range(depth):
        layers.append(_prepare_layer(norm_mix[l], w_in[l], gla_wg_f[l], gla_bg_f[l], gla_wg_b[l], gla_bg_b[l],
                                     gla_norm[l], hy_conv_w[l], hy_conv_b[l], hy_norm[l], w_out[l],
                                     norm_ffn[l], w_gate[l], w_up[l], w_down[l]))
        filters.append(_hyena_filters(feats, lag, hy_w1[l], hy_b1[l], hy_freq1[l], hy_w2[l], hy_b2[l], hy_freq2[l],
                                      hy_w3[l], hy_skip[l], wf, seq=seq, width=width))
    outs = []
    for x in (x_prompt, x_sample):
        for l in range(depth):
            x = _layer(x, layers[l], filters[l], wf, norm_final[None, :] if l == depth - 1 else None)
        outs.append(x)
    return tuple(outs)
```

```python
import functools
import math

import numpy as np
import jax
import jax.numpy as jnp
from jax import lax
from jax.experimental import pallas as pl
from jax.experimental.pallas import tpu as pltpu

f32 = jnp.float32
bf16 = jnp.bfloat16

LANES = 128
SUBLANES = 8
VMEM_LIMIT_BYTES = 56 * 1024 * 1024

NORM_EPS = 1e-6

GLA_HEADS = 4
GLA_DK = 64
GLA_DV = 128
GLA_KW = GLA_HEADS * GLA_DK
GLA_WIDTH = GLA_HEADS * GLA_DV
GLA_GATE_RANK = 16
GLA_GATE_TEMP = 16.0
GLA_CHUNK = 64
GLA_LEVELS = (64, 32, 16)
GLA_SUB = SUBLANES
GLA_UNROLL = 2

HY_ORDER = 2
HY_DIRS = 2
HY_BANDS = 8
HY_EMB = 1 + 2 * HY_BANDS
HY_FFN = 64
HY_FAST_DECAY = 0.3
HY_SLOW_DECAY = 1.5
HY_DECAY_TARGET = 1e-2
HY_PHASES = 4
HY_FTILE = 256
HY_CBLK = 256
HY_BBLK = 2
HY_RB = 16


def _cparams(*sem):
    return pltpu.CompilerParams(dimension_semantics=sem, vmem_limit_bytes=VMEM_LIMIT_BYTES)


def _const_spec(shape):
    nd = len(shape)
    return pl.BlockSpec(shape, lambda *_: (0,) * nd, pipeline_mode=pl.Buffered(1))


def _split_bf16(a):
    hi = a.astype(bf16)
    lo = (a - hi.astype(f32)).astype(bf16)
    return hi, lo


def _dot(a, b):
    return jnp.dot(a, b, preferred_element_type=f32)


def _dot_nt(a, b):
    return lax.dot_general(a, b, (((1,), (1,)), ((), ())), preferred_element_type=f32)


def _dot_tn(a, b):
    return lax.dot_general(a, b, (((0,), (0,)), ((), ())), preferred_element_type=f32)


def _dot3(a_hi, a_lo, b):
    b_hi, b_lo = _split_bf16(b)
    return _dot(a_hi, b_hi) + (_dot(a_lo, b_hi) + _dot(a_hi, b_lo))


def _dot2(a, w):
    a_hi, a_lo = _split_bf16(a)
    return _dot(a_hi, w) + _dot(a_lo, w)


def _rms(x):
    return x * lax.rsqrt(jnp.mean(x * x, axis=-1, keepdims=True) + NORM_EPS)


def _silu(x):
    return x / (1.0 + jnp.exp(-x))


def _dft_kernel(o_ref, *, n_fft, rows):
    half = n_fft // 2
    i = pl.program_id(0)
    n = lax.broadcasted_iota(jnp.int32, (rows, half), 0) + i * rows
    f = lax.broadcasted_iota(jnp.int32, (rows, half), 1)
    ang = ((n * f) & (n_fft - 1)).astype(f32) * (2.0 * math.pi / n_fft)
    o_ref[:, :half] = jnp.cos(ang).astype(o_ref.dtype)
    nyq = jnp.where((n & 1) == 0, 1.0, -1.0)
    im = jnp.where(f == 0, nyq, -jnp.sin(ang))
    o_ref[:, half:] = im.astype(o_ref.dtype)


def _dft_matrix(seq):
    rows = 256
    return pl.pallas_call(
        functools.partial(_dft_kernel, n_fft=2 * seq, rows=rows),
        grid=(seq // rows,),
        out_specs=pl.BlockSpec((rows, 2 * seq), lambda i: (i, 0)),
        out_shape=jax.ShapeDtypeStruct((seq, 2 * seq), bf16),
        compiler_params=_cparams("parallel"),
        name="dft_matrix",
    )()


def _filter_kernel(feats_ref, lag_ref, w1h_ref, w1l_ref, b1_ref, fr1_ref, w2h_ref, w2l_ref, b2_ref, fr2_ref,
                   w3fh_ref, w3fl_ref, w3bh_ref, w3bl_ref, delta_ref, skip_ref, ws_ref,
                   a_ref, b_ref, corr_ref, *, seq):
    nph = HY_PHASES
    ls = seq // nph
    ns = 2 * ls
    feats = feats_ref[...]
    hid = jnp.sin(fr1_ref[...] * (_dot3(w1h_ref[...], w1l_ref[...], feats) + b1_ref[...]))
    hid = jnp.sin(fr2_ref[...] * (_dot3(w2h_ref[...], w2l_ref[...], hid) + b2_ref[...]))
    fwd = _dot3(w3fh_ref[0], w3fl_ref[0], hid)
    bwd = _dot3(w3bh_ref[0], w3bl_ref[0], hid)
    lag = lag_ref[...]
    window = jnp.exp(-(lag * (1.0 / (seq - 1))) * delta_ref[...])
    fwd = fwd * window
    bwd = jnp.where(lag == 0.0, 0.0, bwd * window)
    mu = lax.broadcasted_iota(jnp.int32, (fwd.shape[0], ls), 1)
    first = mu == 0

    def phase(x, p):
        return x[:, p * ls:(p + 1) * ls]

    def delayed(x):
        return jnp.where(first, 0.0, pltpu.roll(x, 1, axis=1))

    skip = skip_ref[0]
    lane = lax.broadcasted_iota(jnp.int32, (fwd.shape[0], LANES), 1)
    for d in range(-(nph - 1), nph):
        if d >= 0:
            pos = phase(fwd, d)
            neg = phase(bwd, 0) if d == 0 else delayed(phase(bwd, nph - d))
        else:
            pos = jnp.where(first, phase(bwd, -d), delayed(phase(fwd, nph + d)))
            neg = jnp.where(first, 0.0, phase(bwd, -d))
        sym = pos + neg
        g_re = _dot2(sym, ws_ref[:, :ls])
        g_im = _dot2(pos - neg, ws_ref[:, ls:])
        g_ny = _dot2(sym, ws_ref[:, ls:ls + LANES])[:, 0:1]
        if d == 0:
            g_re = g_re + skip
            g_ny = g_ny + skip
        a = g_re * jnp.where(first, 1.0 / ns, 2.0 / ns)
        a_ref[0, d + nph - 1] = a
        b_ref[0, d + nph - 1] = jnp.where(first, 0.0, g_im * (2.0 / ns))
        corr_ref[0, d + nph - 1] = jnp.where(lane == 0, g_ny * (1.0 / ns) - a[:, 0:1], 0.0)


def _hyena_filters(feats, lag, w1, b1, fr1, w2, b2, fr2, w3, skip, ws, *, seq, width):
    pad = LANES
    nd = 2 * HY_PHASES - 1
    ls = seq // HY_PHASES

    def pad2(m, r, c):
        return jnp.zeros((r, c), f32).at[:m.shape[0], :m.shape[1]].set(m)

    w1h, w1l = _split_bf16(pad2(w1.T, pad, pad))
    w2h, w2l = _split_bf16(pad2(w2.T, pad, pad))
    w3t = pad2(w3.T, w3.shape[1], pad).reshape(HY_ORDER, HY_DIRS, width, pad)
    w3fh, w3fl = _split_bf16(w3t[:, 0])
    w3bh, w3bl = _split_bf16(w3t[:, 1])
    col = lambda v: pad2(v[:, None], pad, 1)
    max_decay = math.log(HY_DECAY_TARGET) / HY_FAST_DECAY
    min_decay = math.log(HY_DECAY_TARGET) / HY_SLOW_DECAY
    delta = jnp.abs(jnp.linspace(min_decay, max_decay, width, dtype=f32))[:, None]
    cb = HY_CBLK
    full = lambda shape: pl.BlockSpec(shape, lambda o, c: (0,) * len(shape))
    w3spec = pl.BlockSpec((1, cb, pad), lambda o, c: (o, c, 0))
    out_spec = pl.BlockSpec((1, nd, cb, ls), lambda o, c: (o, 0, c, 0))
    return pl.pallas_call(
        functools.partial(_filter_kernel, seq=seq),
        grid=(HY_ORDER, width // cb),
        in_specs=[full((pad, seq)), full((1, seq)),
                  full((pad, pad)), full((pad, pad)), full((pad, 1)), full((pad, 1)),
                  full((pad, pad)), full((pad, pad)), full((pad, 1)), full((pad, 1)),
                  w3spec, w3spec, w3spec, w3spec,
                  pl.BlockSpec((cb, 1), lambda o, c: (c, 0)),
                  pl.BlockSpec((1, cb, 1), lambda o, c: (o, c, 0)),
                  _const_spec(ws.shape)],
        out_specs=[out_spec, out_spec, pl.BlockSpec((1, nd, cb, LANES), lambda o, c: (o, 0, c, 0))],
        out_shape=[jax.ShapeDtypeStruct((HY_ORDER, nd, width, ls), f32),
                   jax.ShapeDtypeStruct((HY_ORDER, nd, width, ls), f32),
                   jax.ShapeDtypeStruct((HY_ORDER, nd, width, LANES), f32)],
        compiler_params=_cparams("arbitrary", "arbitrary"),
        name="hyena_filters",
    )(feats, lag, w1h, w1l, col(b1), col(fr1), w2h, w2l, col(b2), col(fr2),
      w3fh, w3fl, w3bh, w3bl, delta, skip[:, :, None], ws)


def _positional_features(seq):
    ls = seq // HY_PHASES
    col = np.arange(seq)
    lag = jnp.asarray(HY_PHASES * (col % ls) + col // ls, f32)[:, None]
    t = lag * (1.0 / (seq - 1))
    pos_w = (2.0 * math.pi / seq) * lag
    bands = jnp.linspace(1e-4, HY_BANDS - 1.0, HY_BANDS, dtype=f32)
    feats = jnp.concatenate([t, jnp.cos(bands * pos_w), -jnp.sin(bands * pos_w)], axis=-1)
    feats_t = jnp.zeros((LANES, seq), f32).at[:HY_EMB].set(feats.T)
    return feats_t, lag.T


def _in_proj_kernel(x_ref, nw_ref, wa_ref, wlr_ref, wt_ref,
                    q_ref, k_ref, v_ref, r_ref, lr_ref, x2_ref, vt_ref, hs_ref):
    nph = HY_PHASES
    tm = x_ref.shape[1]
    tmr = tm // nph
    hf = _rms(x_ref[0]) * nw_ref[...]
    h = hf.astype(bf16)
    n_tok = wa_ref.shape[1] - x2_ref.shape[-1]
    pa = _dot(h, wa_ref[:, :n_tok])
    o = 0
    for ref in (q_ref, k_ref, v_ref, r_ref):
        w = ref.shape[-1]
        ref[0] = pa[:, o:o + w].astype(ref.dtype)
        o += w
    lr_ref[0] = _dot(h, wlr_ref[...])
    n_slabs = hs_ref.shape[0]
    for c in range(n_slabs):
        hs_ref[c] = hf[:, c * LANES:(c + 1) * LANES]
    hp = jnp.concatenate(
        [jnp.concatenate([hs_ref[c, pl.ds(r, tmr, stride=nph), :] for r in range(nph)], axis=0)
         for c in range(n_slabs)], axis=1).astype(bf16)
    x2 = _dot(hp, wa_ref[:, n_tok:]).astype(x2_ref.dtype)
    vt = _dot_nt(wt_ref[...], hp).astype(vt_ref.dtype)
    for r in range(nph):
        x2_ref[0, r] = x2[r * tmr:(r + 1) * tmr, :]
        vt_ref[0, r] = vt[:, r * tmr:(r + 1) * tmr]


def _in_proj(x, nw, wa, wlr, wt, *, tm):
    bsz, seq, d = x.shape
    nph = HY_PHASES
    hw = wt.shape[0] // 2
    sr, tmr = seq // nph, tm // nph
    row = lambda w: pl.BlockSpec((1, tm, w), lambda b, t: (b, t, 0))
    tok = lambda w, dt: jax.ShapeDtypeStruct((bsz, seq, w), dt)
    return pl.pallas_call(
        _in_proj_kernel,
        grid=(bsz, seq // tm),
        in_specs=[row(d), _const_spec(nw.shape), _const_spec(wa.shape), _const_spec(wlr.shape),
                  _const_spec(wt.shape)],
        out_specs=[row(GLA_KW), row(GLA_KW), row(GLA_WIDTH), row(GLA_WIDTH), row(LANES),
                   pl.BlockSpec((1, nph, tmr, hw), lambda b, t: (b, 0, t, 0)),
                   pl.BlockSpec((1, nph, 2 * hw, tmr), lambda b, t: (b, 0, 0, t))],
        out_shape=[tok(GLA_KW, bf16), tok(GLA_KW, bf16), tok(GLA_WIDTH, bf16), tok(GLA_WIDTH, bf16),
                   tok(LANES, f32),
                   jax.ShapeDtypeStruct((bsz, nph, sr, hw), bf16),
                   jax.ShapeDtypeStruct((bsz, nph, 2 * hw, sr), bf16)],
        scratch_shapes=[pltpu.VMEM((d // LANES, tm, LANES), f32)],
        compiler_params=_cparams("parallel", "parallel"),
        name="in_proj",
    )(x, nw, wa, wlr, wt)


def _block_diag(n_blocks, nr, nc):
    rr = np.arange(n_blocks * nr)[:, None] // nr
    cc = np.arange(n_blocks * nc)[None, :] // nc
    return (rr == cc).astype(np.float32)


def _gla_constants(rev):
    c, s, h = GLA_CHUNK, GLA_SUB, GLA_HEADS
    i = np.arange(c)[:, None]
    j = np.arange(c)[None, :]
    tri = (j >= i) if rev else (j <= i)
    lane_j = np.arange(h * c)[None, :] % c
    query_rows = np.zeros((c, len(GLA_LEVELS)), np.float32)
    same = np.zeros((len(GLA_LEVELS), c, h * c), np.float32)
    for l, size in enumerate(GLA_LEVELS):
        upper = (i % size) >= size // 2
        query_rows[:, l:l + 1] = ~upper if rev else upper
        same[l] = (i // size) == (lane_j // size)
    lag = np.zeros((s, c, h * c), np.float32)
    for d in range(s):
        if rev:
            lag[d] = (lane_j == i + d) & (i % s + d <= s - 1)
        else:
            lag[d] = (lane_j == i - d) & (i % s >= d)
    return tri.astype(np.float32), query_rows, same[1:], lag


def _interleave(*stages):
    live = list(stages)
    while live:
        for gen in list(live):
            try:
                next(gen)
            except StopIteration:
                live.remove(gen)


def _gla_chunk(t0, rev, q_ref, k_ref, v_ref, g_ref, st_ref, o_ref, p_ref, sin_ref,
               tri_ref, rows_ref, same_ref, lag_ref, bd2_ref, bd4_ref, ones_bd_ref):
    c, s, nh = GLA_CHUNK, GLA_SUB, GLA_HEADS
    kw = GLA_KW
    q = q_ref[0, pl.ds(t0, c), :].astype(f32)
    k = k_ref[0, pl.ds(t0, c), :].astype(f32)
    v = v_ref[0, pl.ds(t0, c), :]
    g = g_ref[pl.ds(t0, c), :]
    g_hi, g_lo = _split_bf16(g)
    tri = tri_ref[...]
    b = _dot(tri, g_hi) + _dot(tri, g_lo)
    yield
    b_end = b[0:1] if rev else b[c - 1:c]

    def stack_heads(x):
        return jnp.concatenate([x] * nh, axis=0)

    st = st_ref[...]
    sin_ref[...] = st.astype(bf16)
    bd4 = bd4_ref[...]
    vst = jnp.concatenate([v[:, h * GLA_DV:(h + 1) * GLA_DV] for h in range(nh)], axis=0)
    kt_bd = stack_heads((k * jnp.exp(b_end - b)).astype(bf16)) * bd4
    st_ref[...] = st * jnp.exp(b_end) + _dot_tn(vst, kt_bd)
    yield

    a = None
    for l, size in enumerate(GLA_LEVELS):
        half = size // 2
        ref_rows = [blk * size + (half if rev else half - 1) for blk in range(c // size)]
        ref = jnp.concatenate([jnp.broadcast_to(b[r:r + 1, :], (size, kw)) for r in ref_rows], axis=0)
        is_q = rows_ref[:, l:l + 1] > 0.5
        e = jnp.exp(jnp.minimum(jnp.where(is_q, b - ref, ref - b), 0.0))
        zero = jnp.zeros_like(q)
        ql = jnp.where(is_q, q * e, zero).astype(bf16)
        kl = jnp.where(is_q, zero, k * e).astype(bf16)
        parts = []
        for p in range(kw // LANES):
            sl = slice(p * LANES, (p + 1) * LANES)
            kbd = jnp.concatenate([kl[:, sl]] * 2, axis=0) * bd2_ref[...]
            parts.append(_dot_nt(ql[:, sl], kbd))
        al = jnp.concatenate(parts, axis=1)
        if l > 0:
            al = al * same_ref[l - 1]
        a = al if a is None else a + al
        yield

    gam = jnp.exp(g)
    x = k
    for d in range(s):
        if d > 0:
            x3 = x.reshape(c // s, s, kw)
            x = pltpu.roll(x3, (s - 1) if rev else 1, axis=1).reshape(c, kw) * gam
        p_ref[d * c:(d + 1) * c, :] = (q * x).astype(bf16)
    red = _dot(p_ref[...], ones_bd_ref[...])
    yield
    for d in range(s):
        a = a + red[d * c:(d + 1) * c, :] * lag_ref[d]

    a_st = stack_heads(a.astype(bf16)) * bd4
    qs_st = stack_heads((q * jnp.exp(b)).astype(bf16)) * bd4
    o_st = _dot(a_st, vst) + _dot_nt(qs_st, sin_ref[...])
    yield
    for h in range(nh):
        o_ref[pl.ds(t0, c), h * GLA_DV:(h + 1) * GLA_DV] = o_st[h * c:(h + 1) * c, :]


def _gla_kernel(q_ref, k_ref, v_ref, r_ref, lr_ref, wgh_ref, wgl_ref, bg_ref, nw_ref,
                trif_ref, rowsf_ref, samef_ref, lagf_ref, trib_ref, rowsb_ref, sameb_ref, lagb_ref,
                bd2_ref, bd4_ref, ones_bd_ref,
                out_ref,
                gf_ref, gb_ref, of_ref, ob_ref, stf_ref, stb_ref, p_ref, sin_ref, *, seq, rows):
    c = GLA_CHUNK
    kw = GLA_KW

    def gate_rows(i, carry):
        r0 = pl.multiple_of(i * rows, rows)
        lr_hi, lr_lo = _split_bf16(lr_ref[0, pl.ds(r0, rows), :])
        wgh = wgh_ref[...]
        z = _dot(lr_hi, wgh) + (_dot(lr_lo, wgh) + _dot(lr_hi, wgl_ref[...])) + bg_ref[...]
        g = (jnp.minimum(z, 0.0) - jnp.log(1.0 + jnp.exp(-jnp.abs(z)))) * (1.0 / GLA_GATE_TEMP)
        gf_ref[pl.ds(r0, rows), :] = g[:, :kw]
        gb_ref[pl.ds(r0, rows), :] = g[:, kw:]
        return carry

    lax.fori_loop(0, seq // rows, gate_rows, 0)

    stf_ref[...] = jnp.zeros_like(stf_ref)
    stb_ref[...] = jnp.zeros_like(stb_ref)

    consts = (bd2_ref, bd4_ref, ones_bd_ref)
    n_chunks = seq // c

    def scan(i, carry):
        chains = []
        for u in range(GLA_UNROLL):
            n = i * GLA_UNROLL + u
            tf = pl.multiple_of(n * c, c)
            tb = pl.multiple_of((n_chunks - 1 - n) * c, c)
            chains.append(_gla_chunk(tf, False, q_ref, k_ref, v_ref, gf_ref, stf_ref, of_ref,
                                     p_ref.at[2 * u], sin_ref.at[2 * u],
                                     trif_ref, rowsf_ref, samef_ref, lagf_ref, *consts))
            chains.append(_gla_chunk(tb, True, q_ref, k_ref, v_ref, gb_ref, stb_ref, ob_ref,
                                     p_ref.at[2 * u + 1], sin_ref.at[2 * u + 1],
                                     trib_ref, rowsb_ref, sameb_ref, lagb_ref, *consts))
        _interleave(*chains)
        return carry

    lax.fori_loop(0, n_chunks // GLA_UNROLL, scan, 0)

    def finish(i, carry):
        r0 = pl.multiple_of(i * rows, rows)
        o = of_ref[pl.ds(r0, rows), :] + ob_ref[pl.ds(r0, rows), :]
        gate = _silu(r_ref[0, pl.ds(r0, rows), :].astype(f32))
        nw = nw_ref[...]
        for h in range(GLA_HEADS):
            sl = slice(h * GLA_DV, (h + 1) * GLA_DV)
            out_ref[0, pl.ds(r0, rows), sl] = (_rms(o[:, sl]) * nw * gate[:, sl]).astype(out_ref.dtype)
        return carry

    lax.fori_loop(0, seq // rows, finish, 0)


def _gla(q, k, v, r, lr, wgh, wgl, bg, nw):
    bsz, seq, _ = q.shape
    c, s, h = GLA_CHUNK, GLA_SUB, GLA_HEADS
    consts = []
    for rev in (False, True):
        tri, rows, same, lag = _gla_constants(rev)
        consts += [jnp.asarray(tri, bf16), jnp.asarray(rows), jnp.asarray(same), jnp.asarray(lag)]
    consts += [jnp.asarray(_block_diag(2, c, GLA_DK), bf16),
               jnp.asarray(_block_diag(h, c, GLA_DK), bf16),
               jnp.asarray(_block_diag(h, GLA_DK, c), bf16)]
    per_b = lambda w: pl.BlockSpec((1, seq, w), lambda b: (b, 0, 0))
    params = [wgh, wgl, bg, nw]
    return pl.pallas_call(
        functools.partial(_gla_kernel, seq=seq, rows=256),
        grid=(bsz,),
        in_specs=[per_b(GLA_KW), per_b(GLA_KW), per_b(GLA_WIDTH), per_b(GLA_WIDTH), per_b(LANES)]
        + [_const_spec(p.shape) for p in params] + [_const_spec(x.shape) for x in consts],
        out_specs=per_b(GLA_WIDTH),
        out_shape=jax.ShapeDtypeStruct((bsz, seq, GLA_WIDTH), bf16),
        scratch_shapes=[pltpu.VMEM((seq, GLA_KW), f32), pltpu.VMEM((seq, GLA_KW), f32),
                        pltpu.VMEM((seq, GLA_WIDTH), f32), pltpu.VMEM((seq, GLA_WIDTH), f32),
                        pltpu.VMEM((GLA_DV, GLA_KW), f32), pltpu.VMEM((GLA_DV, GLA_KW), f32),
                        pltpu.VMEM((2 * GLA_UNROLL, s * c, GLA_KW), bf16),
                        pltpu.VMEM((2 * GLA_UNROLL, GLA_DV, GLA_KW), bf16)],
        compiler_params=_cparams("parallel"),
        name="gla",
    )(q, k, v, r, lr, *params, *consts)


def _short_conv_phases(u, w, axis):
    nph = len(u)
    n = u[0].shape[axis]
    m = lax.broadcasted_iota(jnp.int32, u[0].shape, axis)
    before_first = jnp.where(m == 0, 0.0, pltpu.roll(u[nph - 1], 1, axis=axis))
    after_last = jnp.where(m == n - 1, 0.0, pltpu.roll(u[0], n - 1, axis=axis))
    out = []
    for r in range(nph):
        prev = u[r - 1] if r > 0 else before_first
        nxt = u[r + 1] if r < nph - 1 else after_last
        out.append(prev * w[0] + u[r] * w[1] + nxt * w[2] + w[3])
    return out


def _hyena_kernel(v_ref, x1_ref, x2_ref, cwt_v_ref, cwt_x1_ref, cw_x2_ref, a_ref, b_ref, corr_ref,
                  wre_ref, wim_ref, out_ref, src_ref, acc1_ref, acc2_ref, u_ref, y_ref, tok_ref):
    order = pl.program_id(2)
    ft = pl.program_id(3)
    n_ft = pl.num_programs(3)
    nb, nph, cb, ls = v_ref.shape
    rows = nb * cb

    rb = HY_RB

    def conv_to_src(u_ref_in, w_ref, gate_ref):
        def body(i, carry):
            bi = i // (cb // rb)
            crow = pl.multiple_of((i % (cb // rb)) * rb, rb)
            row0 = pl.multiple_of(i * rb, rb)
            w = w_ref[pl.ds(crow, rb), :]
            u = [u_ref_in[bi, r, pl.ds(crow, rb), :].astype(f32) for r in range(nph)]
            c = _short_conv_phases(u, [w[:, j:j + 1] for j in range(4)], axis=1)
            for r in range(nph):
                val = c[r] if gate_ref is None else c[r] * gate_ref[r, pl.ds(row0, rb), :]
                src_ref[r, pl.ds(row0, rb), :] = val.astype(bf16)
            return carry
        lax.fori_loop(0, rows // rb, body, 0, unroll=4)

    @pl.when(jnp.logical_and(order == 0, ft == 0))
    def _():
        conv_to_src(v_ref, cwt_v_ref, None)
        acc1_ref[...] = jnp.zeros_like(acc1_ref)

    @pl.when(jnp.logical_and(order == 1, ft == 0))
    def _():
        conv_to_src(x1_ref, cwt_x1_ref, acc1_ref)
        acc2_ref[...] = jnp.zeros_like(acc2_ref)

    wre = wre_ref[...]
    wim = wim_ref[...]
    ftile = wre.shape[1]
    for r in range(nph):
        u_ref[2 * r] = _dot(src_ref[r], wre)
        u_ref[2 * r + 1] = _dot(src_ref[r], wim)
    first_tile = (ft == 0).astype(f32)

    rb = HY_RB

    def combine(i, carry):
        row0 = pl.multiple_of(i * rb, rb)
        crow = pl.multiple_of((i % (cb // rb)) * rb, rb)
        for lane0 in range(0, ftile, LANES):
            cols = slice(lane0, lane0 + LANES)
            u_re = [u_ref[2 * r, pl.ds(row0, rb), cols] for r in range(nph)]
            u_im = [u_ref[2 * r + 1, pl.ds(row0, rb), cols] for r in range(nph)]
            for rp in range(nph):
                y_re = y_im = None
                for r in range(nph):
                    d = rp - r + nph - 1
                    a = a_ref[0, d, pl.ds(crow, rb), cols]
                    b = b_ref[0, d, pl.ds(crow, rb), cols]
                    t_re = u_re[r] * a - u_im[r] * b
                    t_im = u_re[r] * b + u_im[r] * a
                    if lane0 == 0:
                        t_im = t_im + u_im[r] * (corr_ref[0, d, pl.ds(crow, rb), :] * first_tile)
                    y_re = t_re if y_re is None else y_re + t_re
                    y_im = t_im if y_im is None else y_im + t_im
                y_ref[2 * rp, pl.ds(row0, rb), cols] = y_re.astype(bf16)
                y_ref[2 * rp + 1, pl.ds(row0, rb), cols] = y_im.astype(bf16)
        return carry

    lax.fori_loop(0, rows // rb, combine, 0)

    @pl.when(order == 0)
    def _():
        for rp in range(nph):
            acc1_ref[rp] += _dot_nt(y_ref[2 * rp], wre) + _dot_nt(y_ref[2 * rp + 1], wim)

    @pl.when(order == 1)
    def _():
        for rp in range(nph):
            acc2_ref[rp] += _dot_nt(wre, y_ref[2 * rp]) + _dot_nt(wim, y_ref[2 * rp + 1])

    @pl.when(jnp.logical_and(order == 1, ft == n_ft - 1))
    def _():
        w = cw_x2_ref[...]
        taps = [w[i:i + 1] for i in range(4)]
        for i in range(nb):
            x2 = _short_conv_phases([x2_ref[i, r].astype(f32) for r in range(nph)], taps, axis=0)
            for r in range(nph):
                y = x2[r] * acc2_ref[r][:, i * cb:(i + 1) * cb]
                for c in range(cb // LANES):
                    tok_ref[c, pl.ds(r, ls, stride=nph), :] = y[:, c * LANES:(c + 1) * LANES]
            for c in range(cb // LANES):
                out_ref[i, :, c * LANES:(c + 1) * LANES] = tok_ref[c].astype(out_ref.dtype)


def _hyena(vx1t, x2, cwt, cw_x2, fa, fb, fcorr, ws):
    bsz, nph, two_w, ls = vx1t.shape
    width = two_w // 2
    cb, nb, ftile = HY_CBLK, HY_BBLK, HY_FTILE
    n_cb = width // cb
    n_ft = ls // ftile
    nd = fa.shape[1]
    rows = nb * cb
    filt = lambda w: pl.BlockSpec((1, nd, cb, w), lambda c, b, o, f: (o, 0, c, f))
    return pl.pallas_call(
        _hyena_kernel,
        grid=(n_cb, bsz // nb, HY_ORDER, n_ft),
        in_specs=[pl.BlockSpec((nb, nph, cb, ls), lambda c, b, o, f: (b, 0, c, 0)),
                  pl.BlockSpec((nb, nph, cb, ls), lambda c, b, o, f: (b, 0, n_cb + c, 0)),
                  pl.BlockSpec((nb, nph, ls, cb), lambda c, b, o, f: (b, 0, 0, c)),
                  pl.BlockSpec((cb, 4), lambda c, b, o, f: (c, 0)),
                  pl.BlockSpec((cb, 4), lambda c, b, o, f: (n_cb + c, 0)),
                  pl.BlockSpec((4, cb), lambda c, b, o, f: (0, c)),
                  filt(ftile), filt(ftile),
                  pl.BlockSpec((1, nd, cb, LANES), lambda c, b, o, f: (o, 0, c, 0)),
                  pl.BlockSpec((ls, ftile), lambda c, b, o, f: (0, f)),
                  pl.BlockSpec((ls, ftile), lambda c, b, o, f: (0, n_ft + f))],
        out_specs=pl.BlockSpec((nb, nph * ls, cb), lambda c, b, o, f: (b, 0, c)),
        out_shape=jax.ShapeDtypeStruct((bsz, nph * ls, width), bf16),
        scratch_shapes=[pltpu.VMEM((nph, rows, ls), bf16), pltpu.VMEM((nph, rows, ls), f32),
                        pltpu.VMEM((nph, ls, rows), f32),
                        pltpu.VMEM((2 * nph, rows, ftile), f32), pltpu.VMEM((2 * nph, rows, ftile), bf16),
                        pltpu.VMEM((cb // LANES, nph * ls, LANES), f32)],
        compiler_params=_cparams("parallel", "parallel", "arbitrary", "arbitrary"),
        name="hyena",
    )(vx1t, vx1t, x2, cwt, cwt, cw_x2, fa, fb, fcorr, ws, ws)


def _out_ffn_kernel(x_ref, og_ref, hy_ref, hnw_ref, wo_g_ref, wo_h_ref, fnw_ref, wg_ref, wu_ref, wd_ref,
                    *rest, final_norm):
    if final_norm:
        lnw_ref, out_ref = rest
    else:
        (out_ref,) = rest
    hy = (_rms(hy_ref[...].astype(f32)) * hnw_ref[...]).astype(bf16)
    x = x_ref[...] + _dot(og_ref[...], wo_g_ref[...]) + _dot(hy, wo_h_ref[...])
    h = (_rms(x) * fnw_ref[...]).astype(bf16)
    ff = (_silu(_dot(h, wg_ref[...])) * _dot(h, wu_ref[...])).astype(bf16)
    x = x + _dot(ff, wd_ref[...])
    if final_norm:
        x = _rms(x) * lnw_ref[...]
    out_ref[...] = x


def _out_ffn(x, og, hy, hnw, wo_g, wo_h, fnw, wg, wu, wd, lnw, *, tm):
    n, d = x.shape
    row = lambda w: pl.BlockSpec((tm, w), lambda i: (i, 0))
    weights = [hnw, wo_g, wo_h, fnw, wg, wu, wd] + ([] if lnw is None else [lnw])
    return pl.pallas_call(
        functools.partial(_out_ffn_kernel, final_norm=lnw is not None),
        grid=(n // tm,),
        in_specs=[row(d), row(og.shape[1]), row(hy.shape[1])] + [_const_spec(w.shape) for w in weights],
        out_specs=row(d),
        out_shape=jax.ShapeDtypeStruct((n, d), f32),
        compiler_params=_cparams("parallel"),
        name="out_ffn",
    )(x, og, hy, *weights)


def _prepare_layer(norm_mix, w_in, wg_f, bg_f, wg_b, bg_b, gla_norm, conv_w, conv_b, hy_norm, w_out,
                   norm_ffn, w_gate, w_up, w_down):
    d = w_in.shape[0]
    hw = hy_norm.shape[0]
    o_lr = 2 * GLA_KW + 2 * GLA_WIDTH
    o_hy = o_lr + 2 * GLA_GATE_RANK
    wq = w_in[:, :GLA_KW] * (GLA_DK ** -0.5)
    wa = jnp.concatenate([wq, w_in[:, GLA_KW:o_lr], w_in[:, o_hy + 2 * hw:]], axis=1).astype(bf16)
    wlr = jnp.zeros((d, LANES), f32).at[:, :2 * GLA_GATE_RANK].set(w_in[:, o_lr:o_hy]).astype(bf16)
    wt = w_in[:, o_hy:o_hy + 2 * hw].T.astype(bf16)
    wg = jnp.zeros((LANES, 2 * GLA_KW), f32)
    wg = wg.at[:GLA_GATE_RANK, :GLA_KW].set(wg_f).at[GLA_GATE_RANK:2 * GLA_GATE_RANK, GLA_KW:].set(wg_b)
    wgh, wgl = _split_bf16(wg)
    bg = jnp.concatenate([bg_f, bg_b])[None, :]
    cw = jnp.concatenate([conv_w, conv_b[None, :]], axis=0)
    return dict(
        nw=norm_mix[None, :], wa=wa, wlr=wlr, wt=wt, wgh=wgh, wgl=wgl, bg=bg, gnw=gla_norm[None, :],
        cwt=cw[:, :2 * hw].T, cw_x2=cw[:, 2 * hw:], hnw=hy_norm[None, :],
        wo_g=w_out[:GLA_WIDTH].astype(bf16), wo_h=w_out[GLA_WIDTH:].astype(bf16),
        fnw=norm_ffn[None, :], wg=w_gate.astype(bf16), wu=w_up.astype(bf16), wd=w_down.astype(bf16))


def _layer(x, p, filt, wf, lnw):
    q, k, v, r, lr, x2, vx1t = _in_proj(x, p["nw"], p["wa"], p["wlr"], p["wt"], tm=512)
    og = _gla(q, k, v, r, lr, p["wgh"], p["wgl"], p["bg"], p["gnw"])
    hy = _hyena(vx1t, x2, p["cwt"], p["cw_x2"], *filt, wf)
    bsz, seq, d = x.shape
    n = bsz * seq
    y = _out_ffn(x.reshape(n, d), og.reshape(n, -1), hy.reshape(n, -1), p["hnw"], p["wo_g"], p["wo_h"],
                 p["fnw"], p["wg"], p["wu"], p["wd"], lnw, tm=512)
    return y.reshape(bsz, seq, d)


def kernel(x_prompt, x_sample, norm_mix, w_in, gla_wg_f, gla_bg_f, gla_wg_b, gla_bg_b, gla_norm, hy_conv_w, hy_conv_b, hy_w1, hy_b1, hy_freq1, hy_w2, hy_b2, hy_freq2, hy_w3, hy_skip, hy_norm, w_out, norm_ffn, w_gate, w_up, w_down, norm_final):
    depth = w_in.shape[0]
    seq = x_prompt.shape[1]
    assert x_sample.shape[1] == seq
    width = hy_norm.shape[1]
    wf = _dft_matrix(seq // HY_PHASES)
    feats, lag = _positional_features(seq)
    layers, filters = [], []
    for l in range(depth):
        layers.append(_prepare_layer(norm_mix[l], w_in[l], gla_wg_f[l], gla_bg_f[l], gla_wg_b[l], gla_bg_b[l],
                                     gla_norm[l], hy_conv_w[l], hy_conv_b[l], hy_norm[l], w_out[l],
                                     norm_ffn[l], w_gate[l], w_up[l], w_down[l]))
        filters.append(_hyena_filters(feats, lag, hy_w1[l], hy_b1[l], hy_freq1[l], hy_w2[l], hy_b2[l], hy_freq2[l],
                                      hy_w3[l], hy_skip[l], wf, seq=seq, width=width))
    outs = []
    for x in (x_prompt, x_sample):
        for l in range(depth):
            x = _layer(x, layers[l], filters[l], wf, norm_final[None, :] if l == depth - 1 else None)
        outs.append(x)
    return tuple(outs)
```

```python
import functools
import math

import numpy as np
import jax
import jax.numpy as jnp
from jax import lax
from jax.experimental import pallas as pl
from jax.experimental.pallas import tpu as pltpu

f32 = jnp.float32
bf16 = jnp.bfloat16

LANES = 128
SUBLANES = 8
VMEM_LIMIT_BYTES = 56 * 1024 * 1024

NORM_EPS = 1e-6

GLA_HEADS = 4
GLA_DK = 64
GLA_DV = 128
GLA_KW = GLA_HEADS * GLA_DK
GLA_WIDTH = GLA_HEADS * GLA_DV
GLA_GATE_RANK = 16
GLA_GATE_TEMP = 16.0
GLA_CHUNK = 64
GLA_LEVELS = (64, 32, 16)
GLA_SUB = SUBLANES
GLA_UNROLL = 4

HY_ORDER = 2
HY_DIRS = 2
HY_BANDS = 8
HY_EMB = 1 + 2 * HY_BANDS
HY_FFN = 64
HY_FAST_DECAY = 0.3
HY_SLOW_DECAY = 1.5
HY_DECAY_TARGET = 1e-2
HY_PHASES = 4
HY_FTILE = 256
HY_CBLK = 256
HY_BBLK = 2
HY_RB = 16


def _cparams(*sem):
    return pltpu.CompilerParams(dimension_semantics=sem, vmem_limit_bytes=VMEM_LIMIT_BYTES)


def _const_spec(shape):
    nd = len(shape)
    return pl.BlockSpec(shape, lambda *_: (0,) * nd, pipeline_mode=pl.Buffered(1))


def _split_bf16(a):
    hi = a.astype(bf16)
    lo = (a - hi.astype(f32)).astype(bf16)
    return hi, lo


def _dot(a, b):
    return jnp.dot(a, b, preferred_element_type=f32)


def _dot_nt(a, b):
    return lax.dot_general(a, b, (((1,), (1,)), ((), ())), preferred_element_type=f32)


def _dot_tn(a, b):
    return lax.dot_general(a, b, (((0,), (0,)), ((), ())), preferred_element_type=f32)


def _dot3(a_hi, a_lo, b):
    b_hi, b_lo = _split_bf16(b)
    return _dot(a_hi, b_hi) + (_dot(a_lo, b_hi) + _dot(a_hi, b_lo))


def _dot2(a, w):
    a_hi, a_lo = _split_bf16(a)
    return _dot(a_hi, w) + _dot(a_lo, w)


def _rms(x):
    return x * lax.rsqrt(jnp.mean(x * x, axis=-1, keepdims=True) + NORM_EPS)


def _silu(x):
    return x / (1.0 + jnp.exp(-x))


def _dft_kernel(o_ref, *, n_fft, rows):
    half = n_fft // 2
    i = pl.program_id(0)
    n = lax.broadcasted_iota(jnp.int32, (rows, half), 0) + i * rows
    f = lax.broadcasted_iota(jnp.int32, (rows, half), 1)
    ang = ((n * f) & (n_fft - 1)).astype(f32) * (2.0 * math.pi / n_fft)
    o_ref[:, :half] = jnp.cos(ang).astype(o_ref.dtype)
    nyq = jnp.where((n & 1) == 0, 1.0, -1.0)
    im = jnp.where(f == 0, nyq, -jnp.sin(ang))
    o_ref[:, half:] = im.astype(o_ref.dtype)


def _dft_matrix(seq):
    rows = 256
    return pl.pallas_call(
        functools.partial(_dft_kernel, n_fft=2 * seq, rows=rows),
        grid=(seq // rows,),
        out_specs=pl.BlockSpec((rows, 2 * seq), lambda i: (i, 0)),
        out_shape=jax.ShapeDtypeStruct((seq, 2 * seq), bf16),
        compiler_params=_cparams("parallel"),
        name="dft_matrix",
    )()


def _filter_kernel(feats_ref, lag_ref, w1h_ref, w1l_ref, b1_ref, fr1_ref, w2h_ref, w2l_ref, b2_ref, fr2_ref,
                   w3fh_ref, w3fl_ref, w3bh_ref, w3bl_ref, delta_ref, skip_ref, ws_ref,
                   a_ref, b_ref, corr_ref, *, seq):
    nph = HY_PHASES
    ls = seq // nph
    ns = 2 * ls
    feats = feats_ref[...]
    hid = jnp.sin(fr1_ref[...] * (_dot3(w1h_ref[...], w1l_ref[...], feats) + b1_ref[...]))
    hid = jnp.sin(fr2_ref[...] * (_dot3(w2h_ref[...], w2l_ref[...], hid) + b2_ref[...]))
    fwd = _dot3(w3fh_ref[0], w3fl_ref[0], hid)
    bwd = _dot3(w3bh_ref[0], w3bl_ref[0], hid)
    lag = lag_ref[...]
    window = jnp.exp(-(lag * (1.0 / (seq - 1))) * delta_ref[...])
    fwd = fwd * window
    bwd = jnp.where(lag == 0.0, 0.0, bwd * window)
    mu = lax.broadcasted_iota(jnp.int32, (fwd.shape[0], ls), 1)
    first = mu == 0

    def phase(x, p):
        return x[:, p * ls:(p + 1) * ls]

    def delayed(x):
        return jnp.where(first, 0.0, pltpu.roll(x, 1, axis=1))

    skip = skip_ref[0]
    lane = lax.broadcasted_iota(jnp.int32, (fwd.shape[0], LANES), 1)
    for d in range(-(nph - 1), nph):
        if d >= 0:
            pos = phase(fwd, d)
            neg = phase(bwd, 0) if d == 0 else delayed(phase(bwd, nph - d))
        else:
            pos = jnp.where(first, phase(bwd, -d), delayed(phase(fwd, nph + d)))
            neg = jnp.where(first, 0.0, phase(bwd, -d))
        sym = pos + neg
        g_re = _dot2(sym, ws_ref[:, :ls])
        g_im = _dot2(pos - neg, ws_ref[:, ls:])
        g_ny = _dot2(sym, ws_ref[:, ls:ls + LANES])[:, 0:1]
        if d == 0:
            g_re = g_re + skip
            g_ny = g_ny + skip
        a = g_re * jnp.where(first, 1.0 / ns, 2.0 / ns)
        a_ref[0, d + nph - 1] = a
        b_ref[0, d + nph - 1] = jnp.where(first, 0.0, g_im * (2.0 / ns))
        corr_ref[0, d + nph - 1] = jnp.where(lane == 0, g_ny * (1.0 / ns) - a[:, 0:1], 0.0)


def _hyena_filters(feats, lag, w1, b1, fr1, w2, b2, fr2, w3, skip, ws, *, seq, width):
    pad = LANES
    nd = 2 * HY_PHASES - 1
    ls = seq // HY_PHASES

    def pad2(m, r, c):
        return jnp.zeros((r, c), f32).at[:m.shape[0], :m.shape[1]].set(m)

    w1h, w1l = _split_bf16(pad2(w1.T, pad, pad))
    w2h, w2l = _split_bf16(pad2(w2.T, pad, pad))
    w3t = pad2(w3.T, w3.shape[1], pad).reshape(HY_ORDER, HY_DIRS, width, pad)
    w3fh, w3fl = _split_bf16(w3t[:, 0])
    w3bh, w3bl = _split_bf16(w3t[:, 1])
    col = lambda v: pad2(v[:, None], pad, 1)
    max_decay = math.log(HY_DECAY_TARGET) / HY_FAST_DECAY
    min_decay = math.log(HY_DECAY_TARGET) / HY_SLOW_DECAY
    delta = jnp.abs(jnp.linspace(min_decay, max_decay, width, dtype=f32))[:, None]
    cb = HY_CBLK
    full = lambda shape: pl.BlockSpec(shape, lambda o, c: (0,) * len(shape))
    w3spec = pl.BlockSpec((1, cb, pad), lambda o, c: (o, c, 0))
    out_spec = pl.BlockSpec((1, nd, cb, ls), lambda o, c: (o, 0, c, 0))
    return pl.pallas_call(
        functools.partial(_filter_kernel, seq=seq),
        grid=(HY_ORDER, width // cb),
        in_specs=[full((pad, seq)), full((1, seq)),
                  full((pad, pad)), full((pad, pad)), full((pad, 1)), full((pad, 1)),
                  full((pad, pad)), full((pad, pad)), full((pad, 1)), full((pad, 1)),
                  w3spec, w3spec, w3spec, w3spec,
                  pl.BlockSpec((cb, 1), lambda o, c: (c, 0)),
                  pl.BlockSpec((1, cb, 1), lambda o, c: (o, c, 0)),
                  _const_spec(ws.shape)],
        out_specs=[out_spec, out_spec, pl.BlockSpec((1, nd, cb, LANES), lambda o, c: (o, 0, c, 0))],
        out_shape=[jax.ShapeDtypeStruct((HY_ORDER, nd, width, ls), f32),
                   jax.ShapeDtypeStruct((HY_ORDER, nd, width, ls), f32),
                   jax.ShapeDtypeStruct((HY_ORDER, nd, width, LANES), f32)],
        compiler_params=_cparams("arbitrary", "arbitrary"),
        name="hyena_filters",
    )(feats, lag, w1h, w1l, col(b1), col(fr1), w2h, w2l, col(b2), col(fr2),
      w3fh, w3fl, w3bh, w3bl, delta, skip[:, :, None], ws)


def _positional_features(seq):
    ls = seq // HY_PHASES
    col = np.arange(seq)
    lag = jnp.asarray(HY_PHASES * (col % ls) + col // ls, f32)[:, None]
    t = lag * (1.0 / (seq - 1))
    pos_w = (2.0 * math.pi / seq) * lag
    bands = jnp.linspace(1e-4, HY_BANDS - 1.0, HY_BANDS, dtype=f32)
    feats = jnp.concatenate([t, jnp.cos(bands * pos_w), -jnp.sin(bands * pos_w)], axis=-1)
    feats_t = jnp.zeros((LANES, seq), f32).at[:HY_EMB].set(feats.T)
    return feats_t, lag.T


def _in_proj_kernel(x_ref, nw_ref, wa_ref, wlr_ref, wt_ref, wg_ref, bg_ref,
                    q_ref, k_ref, v_ref, r_ref, gf_ref, gb_ref, x2_ref, vt_ref, hs_ref):
    nph = HY_PHASES
    tm = x_ref.shape[1]
    tmr = tm // nph
    hf = _rms(x_ref[0]) * nw_ref[...]
    h = hf.astype(bf16)
    z = _dot(_dot(h, wlr_ref[...]).astype(bf16), wg_ref[...]) + bg_ref[...]
    n_tok = wa_ref.shape[1] - x2_ref.shape[-1]
    pa = _dot(h, wa_ref[:, :n_tok])
    g = (jnp.minimum(z, 0.0) - jnp.log(1.0 + jnp.exp(-jnp.abs(z)))) * (1.0 / GLA_GATE_TEMP)
    gf_ref[0] = g[:, :GLA_KW]
    gb_ref[0] = g[:, GLA_KW:]
    o = 0
    for ref in (q_ref, k_ref, v_ref, r_ref):
        w = ref.shape[-1]
        ref[0] = pa[:, o:o + w].astype(ref.dtype)
        o += w
    n_slabs = hs_ref.shape[0]
    for c in range(n_slabs):
        hs_ref[c] = hf[:, c * LANES:(c + 1) * LANES]
    hp = jnp.concatenate(
        [jnp.concatenate([hs_ref[c, pl.ds(r, tmr, stride=nph), :] for r in range(nph)], axis=0)
         for c in range(n_slabs)], axis=1).astype(bf16)
    x2 = _dot(hp, wa_ref[:, n_tok:]).astype(x2_ref.dtype)
    vt = _dot_nt(wt_ref[...], hp).astype(vt_ref.dtype)
    for r in range(nph):
        x2_ref[0, r] = x2[r * tmr:(r + 1) * tmr, :]
        vt_ref[0, r] = vt[:, r * tmr:(r + 1) * tmr]


def _in_proj(x, nw, wa, wlr, wt, wg, bg, *, tm):
    bsz, seq, d = x.shape
    nph = HY_PHASES
    hw = wt.shape[0] // 2
    sr, tmr = seq // nph, tm // nph
    row = lambda w: pl.BlockSpec((1, tm, w), lambda b, t: (b, t, 0))
    tok = lambda w, dt: jax.ShapeDtypeStruct((bsz, seq, w), dt)
    return pl.pallas_call(
        _in_proj_kernel,
        grid=(bsz, seq // tm),
        in_specs=[row(d)] + [_const_spec(w.shape) for w in (nw, wa, wlr, wt, wg, bg)],
        out_specs=[row(GLA_KW), row(GLA_KW), row(GLA_WIDTH), row(GLA_WIDTH), row(GLA_KW), row(GLA_KW),
                   pl.BlockSpec((1, nph, tmr, hw), lambda b, t: (b, 0, t, 0)),
                   pl.BlockSpec((1, nph, 2 * hw, tmr), lambda b, t: (b, 0, 0, t))],
        out_shape=[tok(GLA_KW, bf16), tok(GLA_KW, bf16), tok(GLA_WIDTH, bf16), tok(GLA_WIDTH, bf16),
                   tok(GLA_KW, f32), tok(GLA_KW, f32),
                   jax.ShapeDtypeStruct((bsz, nph, sr, hw), bf16),
                   jax.ShapeDtypeStruct((bsz, nph, 2 * hw, sr), bf16)],
        scratch_shapes=[pltpu.VMEM((d // LANES, tm, LANES), f32)],
        compiler_params=_cparams("parallel", "parallel"),
        name="in_proj",
    )(x, nw, wa, wlr, wt, wg, bg)


def _block_diag(n_blocks, nr, nc):
    rr = np.arange(n_blocks * nr)[:, None] // nr
    cc = np.arange(n_blocks * nc)[None, :] // nc
    return (rr == cc).astype(np.float32)


def _gla_constants(rev):
    c, s, h = GLA_CHUNK, GLA_SUB, GLA_HEADS
    i = np.arange(c)[:, None]
    j = np.arange(c)[None, :]
    tri = (j >= i) if rev else (j <= i)
    lane_j = np.arange(h * c)[None, :] % c
    query_rows = np.zeros((c, len(GLA_LEVELS)), np.float32)
    same = np.zeros((len(GLA_LEVELS), c, h * c), np.float32)
    for l, size in enumerate(GLA_LEVELS):
        upper = (i % size) >= size // 2
        query_rows[:, l:l + 1] = ~upper if rev else upper
        same[l] = (i // size) == (lane_j // size)
    lag = np.zeros((s, c, h * c), np.float32)
    for d in range(s):
        if rev:
            lag[d] = (lane_j == i + d) & (i % s + d <= s - 1)
        else:
            lag[d] = (lane_j == i - d) & (i % s >= d)
    return tri.astype(np.float32), query_rows, same[1:], lag


def _interleave(*stages):
    live = list(stages)
    while live:
        for gen in list(live):
            try:
                next(gen)
            except StopIteration:
                live.remove(gen)


def _gla_chunk(t0, rev, q_ref, k_ref, v_ref, g_ref, st_ref, o_ref, p_ref, sin_ref,
               tri_ref, rows_ref, same_ref, lag_ref, bd2_ref, bd4_ref, ones_bd_ref):
    c, s, nh = GLA_CHUNK, GLA_SUB, GLA_HEADS
    kw = GLA_KW
    q = q_ref[0, pl.ds(t0, c), :].astype(f32)
    k = k_ref[0, pl.ds(t0, c), :].astype(f32)
    v = v_ref[0, pl.ds(t0, c), :]
    g = g_ref[0, pl.ds(t0, c), :]
    g_hi, g_lo = _split_bf16(g)
    tri = tri_ref[...]
    b = _dot(tri, g_hi) + _dot(tri, g_lo)
    yield
    b_end = b[0:1] if rev else b[c - 1:c]

    def stack_heads(x):
        return jnp.concatenate([x] * nh, axis=0)

    st = st_ref[...]
    sin_ref[...] = st.astype(bf16)
    bd4 = bd4_ref[...]
    vst = jnp.concatenate([v[:, h * GLA_DV:(h + 1) * GLA_DV] for h in range(nh)], axis=0)
    kt_bd = stack_heads((k * jnp.exp(b_end - b)).astype(bf16)) * bd4
    st_ref[...] = st * jnp.exp(b_end) + _dot_tn(vst, kt_bd)
    yield

    a = None
    for l, size in enumerate(GLA_LEVELS):
        half = size // 2
        ref_rows = [blk * size + (half if rev else half - 1) for blk in range(c // size)]
        ref = jnp.concatenate([jnp.broadcast_to(b[r:r + 1, :], (size, kw)) for r in ref_rows], axis=0)
        is_q = rows_ref[:, l:l + 1] > 0.5
        e = jnp.exp(jnp.minimum(jnp.where(is_q, b - ref, ref - b), 0.0))
        zero = jnp.zeros_like(q)
        ql = jnp.where(is_q, q * e, zero).astype(bf16)
        kl = jnp.where(is_q, zero, k * e).astype(bf16)
        parts = []
        for p in range(kw // LANES):
            sl = slice(p * LANES, (p + 1) * LANES)
            kbd = jnp.concatenate([kl[:, sl]] * 2, axis=0) * bd2_ref[...]
            parts.append(_dot_nt(ql[:, sl], kbd))
        al = jnp.concatenate(parts, axis=1)
        if l > 0:
            al = al * same_ref[l - 1]
        a = al if a is None else a + al
        yield

    gam = jnp.exp(g)
    x = k
    for d in range(s):
        if d > 0:
            x3 = x.reshape(c // s, s, kw)
            x = pltpu.roll(x3, (s - 1) if rev else 1, axis=1).reshape(c, kw) * gam
        p_ref[d * c:(d + 1) * c, :] = (q * x).astype(bf16)
    red = _dot(p_ref[...], ones_bd_ref[...])
    yield
    for d in range(s):
        a = a + red[d * c:(d + 1) * c, :] * lag_ref[d]

    a_st = stack_heads(a.astype(bf16)) * bd4
    qs_st = stack_heads((q * jnp.exp(b)).astype(bf16)) * bd4
    o_st = _dot(a_st, vst) + _dot_nt(qs_st, sin_ref[...])
    yield
    for h in range(nh):
        o_ref[pl.ds(t0, c), h * GLA_DV:(h + 1) * GLA_DV] = o_st[h * c:(h + 1) * c, :]


def _gla_kernel(q_ref, k_ref, v_ref, gf_ref, gb_ref,
                trif_ref, rowsf_ref, samef_ref, lagf_ref, trib_ref, rowsb_ref, sameb_ref, lagb_ref,
                bd2_ref, bd4_ref, ones_bd_ref,
                out_ref,
                of_ref, ob_ref, stf_ref, stb_ref, p_ref, sin_ref, *, seq, rows):
    c = GLA_CHUNK
    stf_ref[...] = jnp.zeros_like(stf_ref)
    stb_ref[...] = jnp.zeros_like(stb_ref)

    consts = (bd2_ref, bd4_ref, ones_bd_ref)
    n_chunks = seq // c

    def scan(i, carry):
        chains = []
        for u in range(GLA_UNROLL):
            n = i * GLA_UNROLL + u
            tf = pl.multiple_of(n * c, c)
            tb = pl.multiple_of((n_chunks - 1 - n) * c, c)
            chains.append(_gla_chunk(tf, False, q_ref, k_ref, v_ref, gf_ref, stf_ref, of_ref,
                                     p_ref.at[2 * u], sin_ref.at[2 * u],
                                     trif_ref, rowsf_ref, samef_ref, lagf_ref, *consts))
            chains.append(_gla_chunk(tb, True, q_ref, k_ref, v_ref, gb_ref, stb_ref, ob_ref,
                                     p_ref.at[2 * u + 1], sin_ref.at[2 * u + 1],
                                     trib_ref, rowsb_ref, sameb_ref, lagb_ref, *consts))
        _interleave(*chains)
        return carry

    lax.fori_loop(0, n_chunks // GLA_UNROLL, scan, 0)

    def finish(i, carry):
        r0 = pl.multiple_of(i * rows, rows)
        o = of_ref[pl.ds(r0, rows), :] + ob_ref[pl.ds(r0, rows), :]
        out_ref[0, pl.ds(r0, rows), :] = o.astype(out_ref.dtype)
        return carry

    lax.fori_loop(0, seq // rows, finish, 0)


def _gla(q, k, v, gf, gb):
    bsz, seq, _ = q.shape
    c, s, h = GLA_CHUNK, GLA_SUB, GLA_HEADS
    consts = []
    for rev in (False, True):
        tri, rows, same, lag = _gla_constants(rev)
        consts += [jnp.asarray(tri, bf16), jnp.asarray(rows), jnp.asarray(same), jnp.asarray(lag)]
    consts += [jnp.asarray(_block_diag(2, c, GLA_DK), bf16),
               jnp.asarray(_block_diag(h, c, GLA_DK), bf16),
               jnp.asarray(_block_diag(h, GLA_DK, c), bf16)]
    per_b = lambda w: pl.BlockSpec((1, seq, w), lambda b: (b, 0, 0))
    return pl.pallas_call(
        functools.partial(_gla_kernel, seq=seq, rows=256),
        grid=(bsz,),
        in_specs=[per_b(GLA_KW), per_b(GLA_KW), per_b(GLA_WIDTH), per_b(GLA_KW), per_b(GLA_KW)]
        + [_const_spec(x.shape) for x in consts],
        out_specs=per_b(GLA_WIDTH),
        out_shape=jax.ShapeDtypeStruct((bsz, seq, GLA_WIDTH), bf16),
        scratch_shapes=[pltpu.VMEM((seq, GLA_WIDTH), f32), pltpu.VMEM((seq, GLA_WIDTH), f32),
                        pltpu.VMEM((GLA_DV, GLA_KW), f32), pltpu.VMEM((GLA_DV, GLA_KW), f32),
                        pltpu.VMEM((2 * GLA_UNROLL, s * c, GLA_KW), bf16),
                        pltpu.VMEM((2 * GLA_UNROLL, GLA_DV, GLA_KW), bf16)],
        compiler_params=_cparams("parallel"),
        name="gla",
    )(q, k, v, gf, gb, *consts)


def _short_conv_phases(u, w, axis):
    nph = len(u)
    n = u[0].shape[axis]
    m = lax.broadcasted_iota(jnp.int32, u[0].shape, axis)
    before_first = jnp.where(m == 0, 0.0, pltpu.roll(u[nph - 1], 1, axis=axis))
    after_last = jnp.where(m == n - 1, 0.0, pltpu.roll(u[0], n - 1, axis=axis))
    out = []
    for r in range(nph):
        prev = u[r - 1] if r > 0 else before_first
        nxt = u[r + 1] if r < nph - 1 else after_last
        out.append(prev * w[0] + u[r] * w[1] + nxt * w[2] + w[3])
    return out


def _hyena_kernel(v_ref, x1_ref, x2_ref, cwt_v_ref, cwt_x1_ref, cw_x2_ref, a_ref, b_ref, corr_ref,
                  wre_ref, wim_ref, out_ref, src_ref, acc1_ref, acc2_ref, u_ref, y_ref, tok_ref):
    order = pl.program_id(2)
    ft = pl.program_id(3)
    n_ft = pl.num_programs(3)
    nb, nph, cb, ls = v_ref.shape
    rows = nb * cb

    rb = HY_RB

    def conv_to_src(u_ref_in, w_ref, gate_ref):
        def body(i, carry):
            bi = i // (cb // rb)
            crow = pl.multiple_of((i % (cb // rb)) * rb, rb)
            row0 = pl.multiple_of(i * rb, rb)
            w = w_ref[pl.ds(crow, rb), :]
            u = [u_ref_in[bi, r, pl.ds(crow, rb), :].astype(f32) for r in range(nph)]
            c = _short_conv_phases(u, [w[:, j:j + 1] for j in range(4)], axis=1)
            for r in range(nph):
                val = c[r] if gate_ref is None else c[r] * gate_ref[r, pl.ds(row0, rb), :]
                src_ref[r, pl.ds(row0, rb), :] = val.astype(bf16)
            return carry
        lax.fori_loop(0, rows // rb, body, 0, unroll=4)

    @pl.when(jnp.logical_and(order == 0, ft == 0))
    def _():
        conv_to_src(v_ref, cwt_v_ref, None)
        acc1_ref[...] = jnp.zeros_like(acc1_ref)

    @pl.when(jnp.logical_and(order == 1, ft == 0))
    def _():
        conv_to_src(x1_ref, cwt_x1_ref, acc1_ref)
        acc2_ref[...] = jnp.zeros_like(acc2_ref)

    wre = wre_ref[...]
    wim = wim_ref[...]
    ftile = wre.shape[1]
    for r in range(nph):
        u_ref[2 * r] = _dot(src_ref[r], wre)
        u_ref[2 * r + 1] = _dot(src_ref[r], wim)
    first_tile = (ft == 0).astype(f32)

    rb = HY_RB

    def combine(i, carry):
        row0 = pl.multiple_of(i * rb, rb)
        crow = pl.multiple_of((i % (cb // rb)) * rb, rb)
        for lane0 in range(0, ftile, LANES):
            cols = slice(lane0, lane0 + LANES)
            u_re = [u_ref[2 * r, pl.ds(row0, rb), cols] for r in range(nph)]
            u_im = [u_ref[2 * r + 1, pl.ds(row0, rb), cols] for r in range(nph)]
            for rp in range(nph):
                y_re = y_im = None
                for r in range(nph):
                    d = rp - r + nph - 1
                    a = a_ref[0, d, pl.ds(crow, rb), cols]
                    b = b_ref[0, d, pl.ds(crow, rb), cols]
                    t_re = u_re[r] * a - u_im[r] * b
                    t_im = u_re[r] * b + u_im[r] * a
                    if lane0 == 0:
                        t_im = t_im + u_im[r] * (corr_ref[0, d, pl.ds(crow, rb), :] * first_tile)
                    y_re = t_re if y_re is None else y_re + t_re
                    y_im = t_im if y_im is None else y_im + t_im
                y_ref[2 * rp, pl.ds(row0, rb), cols] = y_re.astype(bf16)
                y_ref[2 * rp + 1, pl.ds(row0, rb), cols] = y_im.astype(bf16)
        return carry

    lax.fori_loop(0, rows // rb, combine, 0)

    @pl.when(order == 0)
    def _():
        for rp in range(nph):
            acc1_ref[rp] += _dot_nt(y_ref[2 * rp], wre) + _dot_nt(y_ref[2 * rp + 1], wim)

    @pl.when(order == 1)
    def _():
        for rp in range(nph):
            acc2_ref[rp] += _dot_nt(wre, y_ref[2 * rp]) + _dot_nt(wim, y_ref[2 * rp + 1])

    @pl.when(jnp.logical_and(order == 1, ft == n_ft - 1))
    def _():
        w = cw_x2_ref[...]
        taps = [w[i:i + 1] for i in range(4)]
        for i in range(nb):
            x2 = _short_conv_phases([x2_ref[i, r].astype(f32) for r in range(nph)], taps, axis=0)
            for r in range(nph):
                y = x2[r] * acc2_ref[r][:, i * cb:(i + 1) * cb]
                for c in range(cb // LANES):
                    tok_ref[c, pl.ds(r, ls, stride=nph), :] = y[:, c * LANES:(c + 1) * LANES]
            for c in range(cb // LANES):
                out_ref[i, :, c * LANES:(c + 1) * LANES] = tok_ref[c].astype(out_ref.dtype)


def _hyena(vx1t, x2, cwt, cw_x2, fa, fb, fcorr, ws):
    bsz, nph, two_w, ls = vx1t.shape
    width = two_w // 2
    cb, nb, ftile = HY_CBLK, HY_BBLK, HY_FTILE
    n_cb = width // cb
    n_ft = ls // ftile
    nd = fa.shape[1]
    rows = nb * cb
    filt = lambda w: pl.BlockSpec((1, nd, cb, w), lambda c, b, o, f: (o, 0, c, f))
    return pl.pallas_call(
        _hyena_kernel,
        grid=(n_cb, bsz // nb, HY_ORDER, n_ft),
        in_specs=[pl.BlockSpec((nb, nph, cb, ls), lambda c, b, o, f: (b, 0, c, 0)),
                  pl.BlockSpec((nb, nph, cb, ls), lambda c, b, o, f: (b, 0, n_cb + c, 0)),
                  pl.BlockSpec((nb, nph, ls, cb), lambda c, b, o, f: (b, 0, 0, c)),
                  pl.BlockSpec((cb, 4), lambda c, b, o, f: (c, 0)),
                  pl.BlockSpec((cb, 4), lambda c, b, o, f: (n_cb + c, 0)),
                  pl.BlockSpec((4, cb), lambda c, b, o, f: (0, c)),
                  filt(ftile), filt(ftile),
                  pl.BlockSpec((1, nd, cb, LANES), lambda c, b, o, f: (o, 0, c, 0)),
                  pl.BlockSpec((ls, ftile), lambda c, b, o, f: (0, f)),
                  pl.BlockSpec((ls, ftile), lambda c, b, o, f: (0, n_ft + f))],
        out_specs=pl.BlockSpec((nb, nph * ls, cb), lambda c, b, o, f: (b, 0, c)),
        out_shape=jax.ShapeDtypeStruct((bsz, nph * ls, width), bf16),
        scratch_shapes=[pltpu.VMEM((nph, rows, ls), bf16), pltpu.VMEM((nph, rows, ls), f32),
                        pltpu.VMEM((nph, ls, rows), f32),
                        pltpu.VMEM((2 * nph, rows, ftile), f32), pltpu.VMEM((2 * nph, rows, ftile), bf16),
                        pltpu.VMEM((cb // LANES, nph * ls, LANES), f32)],
        compiler_params=_cparams("parallel", "parallel", "arbitrary", "arbitrary"),
        name="hyena",
    )(vx1t, vx1t, x2, cwt, cwt, cw_x2, fa, fb, fcorr, ws, ws)


def _out_ffn_kernel(x_ref, og_ref, r_ref, hy_ref, gnw_ref, hnw_ref, wo_g_ref, wo_h_ref, fnw_ref, wg_ref, wu_ref,
                    wd_ref, *rest, final_norm):
    if final_norm:
        lnw_ref, out_ref = rest
    else:
        (out_ref,) = rest
    o = og_ref[...].astype(f32)
    gate = _silu(r_ref[...].astype(f32))
    gnw = gnw_ref[...]
    og = jnp.concatenate([_rms(o[:, h * GLA_DV:(h + 1) * GLA_DV]) * gnw for h in range(GLA_HEADS)], axis=1)
    og = (og * gate).astype(bf16)
    hy = (_rms(hy_ref[...].astype(f32)) * hnw_ref[...]).astype(bf16)
    x = x_ref[...] + _dot(og, wo_g_ref[...]) + _dot(hy, wo_h_ref[...])
    h = (_rms(x) * fnw_ref[...]).astype(bf16)
    ff = (_silu(_dot(h, wg_ref[...])) * _dot(h, wu_ref[...])).astype(bf16)
    x = x + _dot(ff, wd_ref[...])
    if final_norm:
        x = _rms(x) * lnw_ref[...]
    out_ref[...] = x


def _out_ffn(x, og, r, hy, gnw, hnw, wo_g, wo_h, fnw, wg, wu, wd, lnw, *, tm):
    n, d = x.shape
    row = lambda w: pl.BlockSpec((tm, w), lambda i: (i, 0))
    weights = [gnw, hnw, wo_g, wo_h, fnw, wg, wu, wd] + ([] if lnw is None else [lnw])
    return pl.pallas_call(
        functools.partial(_out_ffn_kernel, final_norm=lnw is not None),
        grid=(n // tm,),
        in_specs=[row(d), row(og.shape[1]), row(r.shape[1]), row(hy.shape[1])]
        + [_const_spec(w.shape) for w in weights],
        out_specs=row(d),
        out_shape=jax.ShapeDtypeStruct((n, d), f32),
        compiler_params=_cparams("parallel"),
        name="out_ffn",
    )(x, og, r, hy, *weights)


def _prepare_layer(norm_mix, w_in, wg_f, bg_f, wg_b, bg_b, gla_norm, conv_w, conv_b, hy_norm, w_out,
                   norm_ffn, w_gate, w_up, w_down):
    d = w_in.shape[0]
    hw = hy_norm.shape[0]
    o_lr = 2 * GLA_KW + 2 * GLA_WIDTH
    o_hy = o_lr + 2 * GLA_GATE_RANK
    wq = w_in[:, :GLA_KW] * (GLA_DK ** -0.5)
    wa = jnp.concatenate([wq, w_in[:, GLA_KW:o_lr], w_in[:, o_hy + 2 * hw:]], axis=1).astype(bf16)
    wlr = jnp.zeros((d, LANES), f32).at[:, :2 * GLA_GATE_RANK].set(w_in[:, o_lr:o_hy]).astype(bf16)
    wt = w_in[:, o_hy:o_hy + 2 * hw].T.astype(bf16)
    wg = jnp.zeros((LANES, 2 * GLA_KW), f32)
    wg = wg.at[:GLA_GATE_RANK, :GLA_KW].set(wg_f).at[GLA_GATE_RANK:2 * GLA_GATE_RANK, GLA_KW:].set(wg_b)
    wg = wg.astype(bf16)
    bg = jnp.concatenate([bg_f, bg_b])[None, :]
    cw = jnp.concatenate([conv_w, conv_b[None, :]], axis=0)
    return dict(
        nw=norm_mix[None, :], wa=wa, wlr=wlr, wt=wt, wgate=wg, bg=bg, gnw=gla_norm[None, :],
        cwt=cw[:, :2 * hw].T, cw_x2=cw[:, 2 * hw:], hnw=hy_norm[None, :],
        wo_g=w_out[:GLA_WIDTH].astype(bf16), wo_h=w_out[GLA_WIDTH:].astype(bf16),
        fnw=norm_ffn[None, :], wg=w_gate.astype(bf16), wu=w_up.astype(bf16), wd=w_down.astype(bf16))


def _layer(x, p, filt, wf, lnw):
    q, k, v, r, gf, gb, x2, vx1t = _in_proj(x, p["nw"], p["wa"], p["wlr"], p["wt"], p["wgate"], p["bg"],
                                            tm=512)
    og = _gla(q, k, v, gf, gb)
    hy = _hyena(vx1t, x2, p["cwt"], p["cw_x2"], *filt, wf)
    bsz, seq, d = x.shape
    n = bsz * seq
    y = _out_ffn(x.reshape(n, d), og.reshape(n, -1), r.reshape(n, -1), hy.reshape(n, -1), p["gnw"], p["hnw"],
                 p["wo_g"], p["wo_h"], p["fnw"], p["wg"], p["wu"], p["wd"], lnw, tm=512)
    return y.reshape(bsz, seq, d)


def kernel(x_prompt, x_sample, norm_mix, w_in, gla_wg_f, gla_bg_f, gla_wg_b, gla_bg_b, gla_norm, hy_conv_w, hy_conv_b, hy_w1, hy_b1, hy_freq1, hy_w2, hy_b2, hy_freq2, hy_w3, hy_skip, hy_norm, w_out, norm_ffn, w_gate, w_up, w_down, norm_final):
    depth = w_in.shape[0]
    seq = x_prompt.shape[1]
    assert x_sample.shape[1] == seq
    width = hy_norm.shape[1]
    wf = _dft_matrix(seq // HY_PHASES)
    feats, lag = _positional_features(seq)
    layers, filters = [], []
    for l in range(depth):
        layers.append(_prepare_layer(norm_mix[l], w_in[l], gla_wg_f[l], gla_bg_f[l], gla_wg_b[l], gla_bg_b[l],
                                     gla_norm[l], hy_conv_w[l], hy_conv_b[l], hy_norm[l], w_out[l],
                                     norm_ffn[l], w_gate[l], w_up[l], w_down[l]))
        filters.append(_hyena_filters(feats, lag, hy_w1[l], hy_b1[l], hy_freq1[l], hy_w2[l], hy_b2[l], hy_freq2[l],
                                      hy_w3[l], hy_skip[l], wf, seq=seq, width=width))
    outs = []
    for x in (x_prompt, x_sample):
        for l in range(depth):
            x = _layer(x, layers[l], filters[l], wf, norm_final[None, :] if l == depth - 1 else None)
        outs.append(x)
    return tuple(outs)
```

```python
import functools
import math

import numpy as np
import jax
import jax.numpy as jnp
from jax import lax
from jax.experimental import pallas as pl
from jax.experimental.pallas import tpu as pltpu

f32 = jnp.float32
bf16 = jnp.bfloat16

LANES = 128
SUBLANES = 8
VMEM_LIMIT_BYTES = 56 * 1024 * 1024

NORM_EPS = 1e-6

GLA_HEADS = 4
GLA_DK = 64
GLA_DV = 128
GLA_KW = GLA_HEADS * GLA_DK
GLA_WIDTH = GLA_HEADS * GLA_DV
GLA_GATE_RANK = 16
GLA_GATE_TEMP = 16.0
GLA_CHUNK = 64
GLA_LEVELS = (64, 32, 16)
GLA_SUB = SUBLANES
GLA_UNROLL = 4

HY_ORDER = 2
HY_DIRS = 2
HY_BANDS = 8
HY_EMB = 1 + 2 * HY_BANDS
HY_FFN = 64
HY_FAST_DECAY = 0.3
HY_SLOW_DECAY = 1.5
HY_DECAY_TARGET = 1e-2
HY_PHASES = 4
HY_FTILE = 256
HY_CBLK = 256
HY_BBLK = 2
HY_RB = 32


def _cparams(*sem):
    return pltpu.CompilerParams(dimension_semantics=sem, vmem_limit_bytes=VMEM_LIMIT_BYTES)


def _const_spec(shape):
    nd = len(shape)
    return pl.BlockSpec(shape, lambda *_: (0,) * nd, pipeline_mode=pl.Buffered(1))


def _split_bf16(a):
    hi = a.astype(bf16)
    lo = (a - hi.astype(f32)).astype(bf16)
    return hi, lo


def _dot(a, b):
    return jnp.dot(a, b, preferred_element_type=f32)


def _dot_nt(a, b):
    return lax.dot_general(a, b, (((1,), (1,)), ((), ())), preferred_element_type=f32)


def _dot_tn(a, b):
    return lax.dot_general(a, b, (((0,), (0,)), ((), ())), preferred_element_type=f32)


def _dot3(a_hi, a_lo, b):
    b_hi, b_lo = _split_bf16(b)
    return _dot(a_hi, b_hi) + (_dot(a_lo, b_hi) + _dot(a_hi, b_lo))


def _dot2(a, w):
    a_hi, a_lo = _split_bf16(a)
    return _dot(a_hi, w) + _dot(a_lo, w)


def _rms(x):
    return x * lax.rsqrt(jnp.mean(x * x, axis=-1, keepdims=True) + NORM_EPS)


def _silu(x):
    return x / (1.0 + jnp.exp(-x))


def _dft_kernel(o_ref, *, n_fft, rows):
    half = n_fft // 2
    i = pl.program_id(0)
    n = lax.broadcasted_iota(jnp.int32, (rows, half), 0) + i * rows
    f = lax.broadcasted_iota(jnp.int32, (rows, half), 1)
    ang = ((n * f) & (n_fft - 1)).astype(f32) * (2.0 * math.pi / n_fft)
    o_ref[:, :half] = jnp.cos(ang).astype(o_ref.dtype)
    nyq = jnp.where((n & 1) == 0, 1.0, -1.0)
    im = jnp.where(f == 0, nyq, -jnp.sin(ang))
    o_ref[:, half:] = im.astype(o_ref.dtype)


def _dft_matrix(seq):
    rows = 256
    return pl.pallas_call(
        functools.partial(_dft_kernel, n_fft=2 * seq, rows=rows),
        grid=(seq // rows,),
        out_specs=pl.BlockSpec((rows, 2 * seq), lambda i: (i, 0)),
        out_shape=jax.ShapeDtypeStruct((seq, 2 * seq), bf16),
        compiler_params=_cparams("parallel"),
        name="dft_matrix",
    )()


def _filter_kernel(feats_ref, lag_ref, w1h_ref, w1l_ref, b1_ref, fr1_ref, w2h_ref, w2l_ref, b2_ref, fr2_ref,
                   w3fh_ref, w3fl_ref, w3bh_ref, w3bl_ref, delta_ref, skip_ref, ws_ref,
                   a_ref, b_ref, corr_ref, *, seq):
    nph = HY_PHASES
    ls = seq // nph
    ns = 2 * ls
    feats = feats_ref[...]
    hid = jnp.sin(fr1_ref[...] * (_dot3(w1h_ref[...], w1l_ref[...], feats) + b1_ref[...]))
    hid = jnp.sin(fr2_ref[...] * (_dot3(w2h_ref[...], w2l_ref[...], hid) + b2_ref[...]))
    fwd = _dot3(w3fh_ref[0], w3fl_ref[0], hid)
    bwd = _dot3(w3bh_ref[0], w3bl_ref[0], hid)
    lag = lag_ref[...]
    window = jnp.exp(-(lag * (1.0 / (seq - 1))) * delta_ref[...])
    fwd = fwd * window
    bwd = jnp.where(lag == 0.0, 0.0, bwd * window)
    mu = lax.broadcasted_iota(jnp.int32, (fwd.shape[0], ls), 1)
    first = mu == 0

    def phase(x, p):
        return x[:, p * ls:(p + 1) * ls]

    def delayed(x):
        return jnp.where(first, 0.0, pltpu.roll(x, 1, axis=1))

    skip = skip_ref[0]
    lane = lax.broadcasted_iota(jnp.int32, (fwd.shape[0], LANES), 1)
    for d in range(-(nph - 1), nph):
        if d >= 0:
            pos = phase(fwd, d)
            neg = phase(bwd, 0) if d == 0 else delayed(phase(bwd, nph - d))
        else:
            pos = jnp.where(first, phase(bwd, -d), delayed(phase(fwd, nph + d)))
            neg = jnp.where(first, 0.0, phase(bwd, -d))
        sym = pos + neg
        g_re = _dot2(sym, ws_ref[:, :ls])
        g_im = _dot2(pos - neg, ws_ref[:, ls:])
        g_ny = _dot2(sym, ws_ref[:, ls:ls + LANES])[:, 0:1]
        if d == 0:
            g_re = g_re + skip
            g_ny = g_ny + skip
        a = g_re * jnp.where(first, 1.0 / ns, 2.0 / ns)
        a_ref[0, d + nph - 1] = a.astype(a_ref.dtype)
        b_ref[0, d + nph - 1] = jnp.where(first, 0.0, g_im * (2.0 / ns)).astype(b_ref.dtype)
        corr = jnp.where(lane == 0, g_ny * (1.0 / ns) - a[:, 0:1], 0.0)
        corr_ref[0, d + nph - 1] = corr.astype(corr_ref.dtype)


def _hyena_filters(feats, lag, w1, b1, fr1, w2, b2, fr2, w3, skip, ws, *, seq, width):
    pad = LANES
    nd = 2 * HY_PHASES - 1
    ls = seq // HY_PHASES

    def pad2(m, r, c):
        return jnp.zeros((r, c), f32).at[:m.shape[0], :m.shape[1]].set(m)

    w1h, w1l = _split_bf16(pad2(w1.T, pad, pad))
    w2h, w2l = _split_bf16(pad2(w2.T, pad, pad))
    w3t = pad2(w3.T, w3.shape[1], pad).reshape(HY_ORDER, HY_DIRS, width, pad)
    w3fh, w3fl = _split_bf16(w3t[:, 0])
    w3bh, w3bl = _split_bf16(w3t[:, 1])
    col = lambda v: pad2(v[:, None], pad, 1)
    max_decay = math.log(HY_DECAY_TARGET) / HY_FAST_DECAY
    min_decay = math.log(HY_DECAY_TARGET) / HY_SLOW_DECAY
    delta = jnp.abs(jnp.linspace(min_decay, max_decay, width, dtype=f32))[:, None]
    cb = HY_CBLK
    full = lambda shape: pl.BlockSpec(shape, lambda o, c: (0,) * len(shape))
    w3spec = pl.BlockSpec((1, cb, pad), lambda o, c: (o, c, 0))
    out_spec = pl.BlockSpec((1, nd, cb, ls), lambda o, c: (o, 0, c, 0))
    return pl.pallas_call(
        functools.partial(_filter_kernel, seq=seq),
        grid=(HY_ORDER, width // cb),
        in_specs=[full((pad, seq)), full((1, seq)),
                  full((pad, pad)), full((pad, pad)), full((pad, 1)), full((pad, 1)),
                  full((pad, pad)), full((pad, pad)), full((pad, 1)), full((pad, 1)),
                  w3spec, w3spec, w3spec, w3spec,
                  pl.BlockSpec((cb, 1), lambda o, c: (c, 0)),
                  pl.BlockSpec((1, cb, 1), lambda o, c: (o, c, 0)),
                  _const_spec(ws.shape)],
        out_specs=[out_spec, out_spec, pl.BlockSpec((1, nd, cb, LANES), lambda o, c: (o, 0, c, 0))],
        out_shape=[jax.ShapeDtypeStruct((HY_ORDER, nd, width, ls), bf16),
                   jax.ShapeDtypeStruct((HY_ORDER, nd, width, ls), bf16),
                   jax.ShapeDtypeStruct((HY_ORDER, nd, width, LANES), bf16)],
        compiler_params=_cparams("arbitrary", "arbitrary"),
        name="hyena_filters",
    )(feats, lag, w1h, w1l, col(b1), col(fr1), w2h, w2l, col(b2), col(fr2),
      w3fh, w3fl, w3bh, w3bl, delta, skip[:, :, None], ws)


def _positional_features(seq):
    ls = seq // HY_PHASES
    col = np.arange(seq)
    lag = jnp.asarray(HY_PHASES * (col % ls) + col // ls, f32)[:, None]
    t = lag * (1.0 / (seq - 1))
    pos_w = (2.0 * math.pi / seq) * lag
    bands = jnp.linspace(1e-4, HY_BANDS - 1.0, HY_BANDS, dtype=f32)
    feats = jnp.concatenate([t, jnp.cos(bands * pos_w), -jnp.sin(bands * pos_w)], axis=-1)
    feats_t = jnp.zeros((LANES, seq), f32).at[:HY_EMB].set(feats.T)
    return feats_t, lag.T


def _in_proj_kernel(x_ref, nw_ref, wa_ref, wlr_ref, wt_ref, wg_ref, bg_ref,
                    q_ref, k_ref, v_ref, r_ref, gf_ref, gb_ref, x2_ref, vt_ref, hs_ref):
    nph = HY_PHASES
    tm = x_ref.shape[1]
    tmr = tm // nph
    hf = _rms(x_ref[0]) * nw_ref[...]
    h = hf.astype(bf16)
    z = _dot(_dot(h, wlr_ref[...]).astype(bf16), wg_ref[...]) + bg_ref[...]
    n_tok = wa_ref.shape[1] - x2_ref.shape[-1]
    pa = _dot(h, wa_ref[:, :n_tok])
    g = (jnp.minimum(z, 0.0) - jnp.log(1.0 + jnp.exp(-jnp.abs(z)))) * (1.0 / GLA_GATE_TEMP)
    gf_ref[0] = g[:, :GLA_KW]
    gb_ref[0] = g[:, GLA_KW:]
    o = 0
    for ref in (q_ref, k_ref, v_ref, r_ref):
        w = ref.shape[-1]
        ref[0] = pa[:, o:o + w].astype(ref.dtype)
        o += w
    n_slabs = hs_ref.shape[0]
    for c in range(n_slabs):
        hs_ref[c] = hf[:, c * LANES:(c + 1) * LANES]
    hp = jnp.concatenate(
        [jnp.concatenate([hs_ref[c, pl.ds(r, tmr, stride=nph), :] for r in range(nph)], axis=0)
         for c in range(n_slabs)], axis=1).astype(bf16)
    x2 = _dot(hp, wa_ref[:, n_tok:]).astype(x2_ref.dtype)
    vt = _dot_nt(wt_ref[...], hp).astype(vt_ref.dtype)
    for r in range(nph):
        x2_ref[0, r] = x2[r * tmr:(r + 1) * tmr, :]
        vt_ref[0, r] = vt[:, r * tmr:(r + 1) * tmr]


def _in_proj(x, nw, wa, wlr, wt, wg, bg, *, tm):
    bsz, seq, d = x.shape
    nph = HY_PHASES
    hw = wt.shape[0] // 2
    sr, tmr = seq // nph, tm // nph
    row = lambda w: pl.BlockSpec((1, tm, w), lambda b, t: (b, t, 0))
    tok = lambda w, dt: jax.ShapeDtypeStruct((bsz, seq, w), dt)
    return pl.pallas_call(
        _in_proj_kernel,
        grid=(bsz, seq // tm),
        in_specs=[row(d)] + [_const_spec(w.shape) for w in (nw, wa, wlr, wt, wg, bg)],
        out_specs=[row(GLA_KW), row(GLA_KW), row(GLA_WIDTH), row(GLA_WIDTH), row(GLA_KW), row(GLA_KW),
                   pl.BlockSpec((1, nph, tmr, hw), lambda b, t: (b, 0, t, 0)),
                   pl.BlockSpec((1, nph, 2 * hw, tmr), lambda b, t: (b, 0, 0, t))],
        out_shape=[tok(GLA_KW, bf16), tok(GLA_KW, bf16), tok(GLA_WIDTH, bf16), tok(GLA_WIDTH, bf16),
                   tok(GLA_KW, f32), tok(GLA_KW, f32),
                   jax.ShapeDtypeStruct((bsz, nph, sr, hw), bf16),
                   jax.ShapeDtypeStruct((bsz, nph, 2 * hw, sr), bf16)],
        scratch_shapes=[pltpu.VMEM((d // LANES, tm, LANES), f32)],
        compiler_params=_cparams("parallel", "parallel"),
        name="in_proj",
    )(x, nw, wa, wlr, wt, wg, bg)


def _block_diag(n_blocks, nr, nc):
    rr = np.arange(n_blocks * nr)[:, None] // nr
    cc = np.arange(n_blocks * nc)[None, :] // nc
    return (rr == cc).astype(np.float32)


def _gla_constants(rev):
    c, s, h = GLA_CHUNK, GLA_SUB, GLA_HEADS
    i = np.arange(c)[:, None]
    j = np.arange(c)[None, :]
    tri = (j >= i) if rev else (j <= i)
    lane_j = np.arange(h * c)[None, :] % c
    query_rows = np.zeros((c, len(GLA_LEVELS)), np.float32)
    same = np.zeros((len(GLA_LEVELS), c, h * c), np.float32)
    for l, size in enumerate(GLA_LEVELS):
        upper = (i % size) >= size // 2
        query_rows[:, l:l + 1] = ~upper if rev else upper
        same[l] = (i // size) == (lane_j // size)
    lag = np.zeros((s, c, h * c), np.float32)
    for d in range(s):
        if rev:
            lag[d] = (lane_j == i + d) & (i % s + d <= s - 1)
        else:
            lag[d] = (lane_j == i - d) & (i % s >= d)
    return tri.astype(np.float32), query_rows, same[1:], lag


def _interleave(*stages):
    live = list(stages)
    while live:
        for gen in list(live):
            try:
                next(gen)
            except StopIteration:
                live.remove(gen)


def _gla_chunk(t0, rev, q_ref, k_ref, v_ref, g_ref, st_ref, o_ref, p_ref, sin_ref,
               tri_ref, rows_ref, same_ref, lag_ref, bd2_ref, bd4_ref, ones_bd_ref):
    c, s, nh = GLA_CHUNK, GLA_SUB, GLA_HEADS
    kw = GLA_KW
    q = q_ref[0, pl.ds(t0, c), :].astype(f32)
    k = k_ref[0, pl.ds(t0, c), :].astype(f32)
    v = v_ref[0, pl.ds(t0, c), :]
    g = g_ref[0, pl.ds(t0, c), :]
    g_hi, g_lo = _split_bf16(g)
    tri = tri_ref[...]
    b = _dot(tri, g_hi) + _dot(tri, g_lo)
    yield
    b_end = b[0:1] if rev else b[c - 1:c]

    def stack_heads(x):
        return jnp.concatenate([x] * nh, axis=0)

    st = st_ref[...]
    sin_ref[...] = st.astype(bf16)
    bd4 = bd4_ref[...]
    vst = jnp.concatenate([v[:, h * GLA_DV:(h + 1) * GLA_DV] for h in range(nh)], axis=0)
    kt_bd = stack_heads((k * jnp.exp(b_end - b)).astype(bf16)) * bd4
    st_ref[...] = st * jnp.exp(b_end) + _dot_tn(vst, kt_bd)
    yield

    a = None
    for l, size in enumerate(GLA_LEVELS):
        half = size // 2
        ref_rows = [blk * size + (half if rev else half - 1) for blk in range(c // size)]
        ref = jnp.concatenate([jnp.broadcast_to(b[r:r + 1, :], (size, kw)) for r in ref_rows], axis=0)
        is_q = rows_ref[:, l:l + 1] > 0.5
        e = jnp.exp(jnp.minimum(jnp.where(is_q, b - ref, ref - b), 0.0))
        zero = jnp.zeros_like(q)
        ql = jnp.where(is_q, q * e, zero).astype(bf16)
        kl = jnp.where(is_q, zero, k * e).astype(bf16)
        parts = []
        for p in range(kw // LANES):
            sl = slice(p * LANES, (p + 1) * LANES)
            kbd = jnp.concatenate([kl[:, sl]] * 2, axis=0) * bd2_ref[...]
            parts.append(_dot_nt(ql[:, sl], kbd))
        al = jnp.concatenate(parts, axis=1)
        if l > 0:
            al = al * same_ref[l - 1]
        a = al if a is None else a + al
        yield

    gam = jnp.exp(g)
    x = k
    for d in range(s):
        if d > 0:
            x3 = x.reshape(c // s, s, kw)
            x = pltpu.roll(x3, (s - 1) if rev else 1, axis=1).reshape(c, kw) * gam
        p_ref[d * c:(d + 1) * c, :] = (q * x).astype(bf16)
    red = _dot(p_ref[...], ones_bd_ref[...])
    yield
    for d in range(s):
        a = a + red[d * c:(d + 1) * c, :] * lag_ref[d]

    a_st = stack_heads(a.astype(bf16)) * bd4
    qs_st = stack_heads((q * jnp.exp(b)).astype(bf16)) * bd4
    o_st = _dot(a_st, vst) + _dot_nt(qs_st, sin_ref[...])
    yield
    for h in range(nh):
        o_ref[pl.ds(t0, c), h * GLA_DV:(h + 1) * GLA_DV] = o_st[h * c:(h + 1) * c, :]


def _gla_kernel(q_ref, k_ref, v_ref, gf_ref, gb_ref,
                trif_ref, rowsf_ref, samef_ref, lagf_ref, trib_ref, rowsb_ref, sameb_ref, lagb_ref,
                bd2_ref, bd4_ref, ones_bd_ref,
                out_ref,
                of_ref, ob_ref, stf_ref, stb_ref, p_ref, sin_ref, *, seq, rows):
    c = GLA_CHUNK
    stf_ref[...] = jnp.zeros_like(stf_ref)
    stb_ref[...] = jnp.zeros_like(stb_ref)

    consts = (bd2_ref, bd4_ref, ones_bd_ref)
    n_chunks = seq // c

    def scan(i, carry):
        chains = []
        for u in range(GLA_UNROLL):
            n = i * GLA_UNROLL + u
            tf = pl.multiple_of(n * c, c)
            tb = pl.multiple_of((n_chunks - 1 - n) * c, c)
            chains.append(_gla_chunk(tf, False, q_ref, k_ref, v_ref, gf_ref, stf_ref, of_ref,
                                     p_ref.at[2 * u], sin_ref.at[2 * u],
                                     trif_ref, rowsf_ref, samef_ref, lagf_ref, *consts))
            chains.append(_gla_chunk(tb, True, q_ref, k_ref, v_ref, gb_ref, stb_ref, ob_ref,
                                     p_ref.at[2 * u + 1], sin_ref.at[2 * u + 1],
                                     trib_ref, rowsb_ref, sameb_ref, lagb_ref, *consts))
        _interleave(*chains)
        return carry

    lax.fori_loop(0, n_chunks // GLA_UNROLL, scan, 0)

    def finish(i, carry):
        r0 = pl.multiple_of(i * rows, rows)
        o = of_ref[pl.ds(r0, rows), :] + ob_ref[pl.ds(r0, rows), :]
        out_ref[0, pl.ds(r0, rows), :] = o.astype(out_ref.dtype)
        return carry

    lax.fori_loop(0, seq // rows, finish, 0)


def _gla(q, k, v, gf, gb):
    bsz, seq, _ = q.shape
    c, s, h = GLA_CHUNK, GLA_SUB, GLA_HEADS
    consts = []
    for rev in (False, True):
        tri, rows, same, lag = _gla_constants(rev)
        consts += [jnp.asarray(tri, bf16), jnp.asarray(rows), jnp.asarray(same), jnp.asarray(lag)]
    consts += [jnp.asarray(_block_diag(2, c, GLA_DK), bf16),
               jnp.asarray(_block_diag(h, c, GLA_DK), bf16),
               jnp.asarray(_block_diag(h, GLA_DK, c), bf16)]
    per_b = lambda w: pl.BlockSpec((1, seq, w), lambda b: (b, 0, 0))
    return pl.pallas_call(
        functools.partial(_gla_kernel, seq=seq, rows=256),
        grid=(bsz,),
        in_specs=[per_b(GLA_KW), per_b(GLA_KW), per_b(GLA_WIDTH), per_b(GLA_KW), per_b(GLA_KW)]
        + [_const_spec(x.shape) for x in consts],
        out_specs=per_b(GLA_WIDTH),
        out_shape=jax.ShapeDtypeStruct((bsz, seq, GLA_WIDTH), bf16),
        scratch_shapes=[pltpu.VMEM((seq, GLA_WIDTH), f32), pltpu.VMEM((seq, GLA_WIDTH), f32),
                        pltpu.VMEM((GLA_DV, GLA_KW), f32), pltpu.VMEM((GLA_DV, GLA_KW), f32),
                        pltpu.VMEM((2 * GLA_UNROLL, s * c, GLA_KW), bf16),
                        pltpu.VMEM((2 * GLA_UNROLL, GLA_DV, GLA_KW), bf16)],
        compiler_params=_cparams("parallel"),
        name="gla",
    )(q, k, v, gf, gb, *consts)


def _short_conv_phases(u, w, axis):
    nph = len(u)
    n = u[0].shape[axis]
    m = lax.broadcasted_iota(jnp.int32, u[0].shape, axis)
    before_first = jnp.where(m == 0, 0.0, pltpu.roll(u[nph - 1], 1, axis=axis))
    after_last = jnp.where(m == n - 1, 0.0, pltpu.roll(u[0], n - 1, axis=axis))
    out = []
    for r in range(nph):
        prev = u[r - 1] if r > 0 else before_first
        nxt = u[r + 1] if r < nph - 1 else after_last
        out.append(prev * w[0] + u[r] * w[1] + nxt * w[2] + w[3])
    return out


def _hyena_kernel(v_ref, x1_ref, x2_ref, cwt_v_ref, cwt_x1_ref, cw_x2_ref, a_ref, b_ref, corr_ref,
                  wre_ref, wim_ref, out_ref, src_ref, acc1_ref, acc2_ref, u_ref, y_ref, tok_ref):
    order = pl.program_id(2)
    ft = pl.program_id(3)
    n_ft = pl.num_programs(3)
    nb, nph, cb, ls = v_ref.shape
    rows = nb * cb

    rb = HY_RB

    def conv_to_src(u_ref_in, w_ref, gate_ref):
        def body(i, carry):
            bi = i // (cb // rb)
            crow = pl.multiple_of((i % (cb // rb)) * rb, rb)
            row0 = pl.multiple_of(i * rb, rb)
            w = w_ref[pl.ds(crow, rb), :]
            u = [u_ref_in[bi, r, pl.ds(crow, rb), :].astype(f32) for r in range(nph)]
            c = _short_conv_phases(u, [w[:, j:j + 1] for j in range(4)], axis=1)
            for r in range(nph):
                val = c[r] if gate_ref is None else c[r] * gate_ref[r, pl.ds(row0, rb), :]
                src_ref[r, pl.ds(row0, rb), :] = val.astype(bf16)
            return carry
        lax.fori_loop(0, rows // rb, body, 0, unroll=4)

    @pl.when(jnp.logical_and(order == 0, ft == 0))
    def _():
        conv_to_src(v_ref, cwt_v_ref, None)
        acc1_ref[...] = jnp.zeros_like(acc1_ref)

    @pl.when(jnp.logical_and(order == 1, ft == 0))
    def _():
        conv_to_src(x1_ref, cwt_x1_ref, acc1_ref)
        acc2_ref[...] = jnp.zeros_like(acc2_ref)

    wre = wre_ref[...]
    wim = wim_ref[...]
    ftile = wre.shape[1]
    for r in range(nph):
        u_ref[2 * r] = _dot(src_ref[r], wre).astype(bf16)
        u_ref[2 * r + 1] = _dot(src_ref[r], wim).astype(bf16)
    first_tile = (ft == 0).astype(f32).astype(bf16)

    rb = HY_RB

    def combine(i, carry):
        row0 = pl.multiple_of(i * rb, rb)
        crow = pl.multiple_of((i % (cb // rb)) * rb, rb)
        for lane0 in range(0, ftile, LANES):
            cols = slice(lane0, lane0 + LANES)
            u_re = [u_ref[2 * r, pl.ds(row0, rb), cols] for r in range(nph)]
            u_im = [u_ref[2 * r + 1, pl.ds(row0, rb), cols] for r in range(nph)]
            for rp in range(nph):
                y_re = y_im = None
                for r in range(nph):
                    d = rp - r + nph - 1
                    a = a_ref[0, d, pl.ds(crow, rb), cols]
                    b = b_ref[0, d, pl.ds(crow, rb), cols]
                    t_re = u_re[r] * a - u_im[r] * b
                    t_im = u_re[r] * b + u_im[r] * a
                    if lane0 == 0:
                        t_im = t_im + u_im[r] * (corr_ref[0, d, pl.ds(crow, rb), :] * first_tile)
                    y_re = t_re if y_re is None else y_re + t_re
                    y_im = t_im if y_im is None else y_im + t_im
                y_ref[2 * rp, pl.ds(row0, rb), cols] = y_re
                y_ref[2 * rp + 1, pl.ds(row0, rb), cols] = y_im
        return carry

    lax.fori_loop(0, rows // rb, combine, 0)

    @pl.when(order == 0)
    def _():
        for rp in range(nph):
            acc1_ref[rp] += _dot_nt(y_ref[2 * rp], wre) + _dot_nt(y_ref[2 * rp + 1], wim)

    @pl.when(order == 1)
    def _():
        for rp in range(nph):
            acc2_ref[rp] += _dot_nt(wre, y_ref[2 * rp]) + _dot_nt(wim, y_ref[2 * rp + 1])

    @pl.when(jnp.logical_and(order == 1, ft == n_ft - 1))
    def _():
        w = cw_x2_ref[...]
        taps = [w[i:i + 1] for i in range(4)]
        for i in range(nb):
            x2 = _short_conv_phases([x2_ref[i, r].astype(f32) for r in range(nph)], taps, axis=0)
            for r in range(nph):
                y = x2[r] * acc2_ref[r][:, i * cb:(i + 1) * cb]
                for c in range(cb // LANES):
                    tok_ref[c, pl.ds(r, ls, stride=nph), :] = y[:, c * LANES:(c + 1) * LANES]
            for c in range(cb // LANES):
                out_ref[i, :, c * LANES:(c + 1) * LANES] = tok_ref[c].astype(out_ref.dtype)


def _hyena(vx1t, x2, cwt, cw_x2, fa, fb, fcorr, ws):
    bsz, nph, two_w, ls = vx1t.shape
    width = two_w // 2
    cb, nb, ftile = HY_CBLK, HY_BBLK, HY_FTILE
    n_cb = width // cb
    n_ft = ls // ftile
    nd = fa.shape[1]
    rows = nb * cb
    filt = lambda w: pl.BlockSpec((1, nd, cb, w), lambda c, b, o, f: (o, 0, c, f))
    return pl.pallas_call(
        _hyena_kernel,
        grid=(n_cb, bsz // nb, HY_ORDER, n_ft),
        in_specs=[pl.BlockSpec((nb, nph, cb, ls), lambda c, b, o, f: (b, 0, c, 0)),
                  pl.BlockSpec((nb, nph, cb, ls), lambda c, b, o, f: (b, 0, n_cb + c, 0)),
                  pl.BlockSpec((nb, nph, ls, cb), lambda c, b, o, f: (b, 0, 0, c)),
                  pl.BlockSpec((cb, 4), lambda c, b, o, f: (c, 0)),
                  pl.BlockSpec((cb, 4), lambda c, b, o, f: (n_cb + c, 0)),
                  pl.BlockSpec((4, cb), lambda c, b, o, f: (0, c)),
                  filt(ftile), filt(ftile),
                  pl.BlockSpec((1, nd, cb, LANES), lambda c, b, o, f: (o, 0, c, 0)),
                  pl.BlockSpec((ls, ftile), lambda c, b, o, f: (0, f)),
                  pl.BlockSpec((ls, ftile), lambda c, b, o, f: (0, n_ft + f))],
        out_specs=pl.BlockSpec((nb, nph * ls, cb), lambda c, b, o, f: (b, 0, c)),
        out_shape=jax.ShapeDtypeStruct((bsz, nph * ls, width), bf16),
        scratch_shapes=[pltpu.VMEM((nph, rows, ls), bf16), pltpu.VMEM((nph, rows, ls), f32),
                        pltpu.VMEM((nph, ls, rows), f32),
                        pltpu.VMEM((2 * nph, rows, ftile), bf16), pltpu.VMEM((2 * nph, rows, ftile), bf16),
                        pltpu.VMEM((cb // LANES, nph * ls, LANES), f32)],
        compiler_params=_cparams("parallel", "parallel", "arbitrary", "arbitrary"),
        name="hyena",
    )(vx1t, vx1t, x2, cwt, cwt, cw_x2, fa, fb, fcorr, ws, ws)


def _out_ffn_kernel(x_ref, og_ref, r_ref, hy_ref, gnw_ref, hnw_ref, wo_g_ref, wo_h_ref, fnw_ref, wg_ref, wu_ref,
                    wd_ref, *rest, final_norm):
    if final_norm:
        lnw_ref, out_ref = rest
    else:
        (out_ref,) = rest
    o = og_ref[...].astype(f32)
    gate = _silu(r_ref[...].astype(f32))
    gnw = gnw_ref[...]
    og = jnp.concatenate([_rms(o[:, h * GLA_DV:(h + 1) * GLA_DV]) * gnw for h in range(GLA_HEADS)], axis=1)
    og = (og * gate).astype(bf16)
    hy = (_rms(hy_ref[...].astype(f32)) * hnw_ref[...]).astype(bf16)
    x = x_ref[...] + _dot(og, wo_g_ref[...]) + _dot(hy, wo_h_ref[...])
    h = (_rms(x) * fnw_ref[...]).astype(bf16)
    ff = (_silu(_dot(h, wg_ref[...])) * _dot(h, wu_ref[...])).astype(bf16)
    x = x + _dot(ff, wd_ref[...])
    if final_norm:
        x = _rms(x) * lnw_ref[...]
    out_ref[...] = x


def _out_ffn(x, og, r, hy, gnw, hnw, wo_g, wo_h, fnw, wg, wu, wd, lnw, *, tm):
    n, d = x.shape
    row = lambda w: pl.BlockSpec((tm, w), lambda i: (i, 0))
    weights = [gnw, hnw, wo_g, wo_h, fnw, wg, wu, wd] + ([] if lnw is None else [lnw])
    return pl.pallas_call(
        functools.partial(_out_ffn_kernel, final_norm=lnw is not None),
        grid=(n // tm,),
        in_specs=[row(d), row(og.shape[1]), row(r.shape[1]), row(hy.shape[1])]
        + [_const_spec(w.shape) for w in weights],
        out_specs=row(d),
        out_shape=jax.ShapeDtypeStruct((n, d), f32),
        compiler_params=_cparams("parallel"),
        name="out_ffn",
    )(x, og, r, hy, *weights)


def _prepare_layer(norm_mix, w_in, wg_f, bg_f, wg_b, bg_b, gla_norm, conv_w, conv_b, hy_norm, w_out,
                   norm_ffn, w_gate, w_up, w_down):
    d = w_in.shape[0]
    hw = hy_norm.shape[0]
    o_lr = 2 * GLA_KW + 2 * GLA_WIDTH
    o_hy = o_lr + 2 * GLA_GATE_RANK
    wq = w_in[:, :GLA_KW] * (GLA_DK ** -0.5)
    wa = jnp.concatenate([wq, w_in[:, GLA_KW:o_lr], w_in[:, o_hy + 2 * hw:]], axis=1).astype(bf16)
    wlr = jnp.zeros((d, LANES), f32).at[:, :2 * GLA_GATE_RANK].set(w_in[:, o_lr:o_hy]).astype(bf16)
    wt = w_in[:, o_hy:o_hy + 2 * hw].T.astype(bf16)
    wg = jnp.zeros((LANES, 2 * GLA_KW), f32)
    wg = wg.at[:GLA_GATE_RANK, :GLA_KW].set(wg_f).at[GLA_GATE_RANK:2 * GLA_GATE_RANK, GLA_KW:].set(wg_b)
    wg = wg.astype(bf16)
    bg = jnp.concatenate([bg_f, bg_b])[None, :]
    cw = jnp.concatenate([conv_w, conv_b[None, :]], axis=0)
    return dict(
        nw=norm_mix[None, :], wa=wa, wlr=wlr, wt=wt, wgate=wg, bg=bg, gnw=gla_norm[None, :],
        cwt=cw[:, :2 * hw].T, cw_x2=cw[:, 2 * hw:], hnw=hy_norm[None, :],
        wo_g=w_out[:GLA_WIDTH].astype(bf16), wo_h=w_out[GLA_WIDTH:].astype(bf16),
        fnw=norm_ffn[None, :], wg=w_gate.astype(bf16), wu=w_up.astype(bf16), wd=w_down.astype(bf16))


def _layer(x, p, filt, wf, lnw):
    q, k, v, r, gf, gb, x2, vx1t = _in_proj(x, p["nw"], p["wa"], p["wlr"], p["wt"], p["wgate"], p["bg"],
                                            tm=512)
    og = _gla(q, k, v, gf, gb)
    hy = _hyena(vx1t, x2, p["cwt"], p["cw_x2"], *filt, wf)
    bsz, seq, d = x.shape
    n = bsz * seq
    y = _out_ffn(x.reshape(n, d), og.reshape(n, -1), r.reshape(n, -1), hy.reshape(n, -1), p["gnw"], p["hnw"],
                 p["wo_g"], p["wo_h"], p["fnw"], p["wg"], p["wu"], p["wd"], lnw, tm=512)
    return y.reshape(bsz, seq, d)


def kernel(x_prompt, x_sample, norm_mix, w_in, gla_wg_f, gla_bg_f, gla_wg_b, gla_bg_b, gla_norm, hy_conv_w, hy_conv_b, hy_w1, hy_b1, hy_freq1, hy_w2, hy_b2, hy_freq2, hy_w3, hy_skip, hy_norm, w_out, norm_ffn, w_gate, w_up, w_down, norm_final):
    depth = w_in.shape[0]
    seq = x_prompt.shape[1]
    assert x_sample.shape[1] == seq
    width = hy_norm.shape[1]
    wf = _dft_matrix(seq // HY_PHASES)
    feats, lag = _positional_features(seq)
    layers, filters = [], []
    for l in range(depth):
        layers.append(_prepare_layer(norm_mix[l], w_in[l], gla_wg_f[l], gla_bg_f[l], gla_wg_b[l], gla_bg_b[l],
                                     gla_norm[l], hy_conv_w[l], hy_conv_b[l], hy_norm[l], w_out[l],
                                     norm_ffn[l], w_gate[l], w_up[l], w_down[l]))
        filters.append(_hyena_filters(feats, lag, hy_w1[l], hy_b1[l], hy_freq1[l], hy_w2[l], hy_b2[l], hy_freq2[l],
                                      hy_w3[l], hy_skip[l], wf, seq=seq, width=width))
    outs = []
    for x in (x_prompt, x_sample):
        for l in range(depth):
            x = _layer(x, layers[l], filters[l], wf, norm_final[None, :] if l == depth - 1 else None)
        outs.append(x)
    return tuple(outs)
```

```python
import functools
import math

import numpy as np
import jax
import jax.numpy as jnp
from jax import lax
from jax.experimental import pallas as pl
from jax.experimental.pallas import tpu as pltpu

f32 = jnp.float32
bf16 = jnp.bfloat16

LANES = 128
SUBLANES = 8
VMEM_LIMIT_BYTES = 56 * 1024 * 1024

NORM_EPS = 1e-6

GLA_HEADS = 4
GLA_DK = 64
GLA_DV = 128
GLA_KW = GLA_HEADS * GLA_DK
GLA_WIDTH = GLA_HEADS * GLA_DV
GLA_GATE_RANK = 16
GLA_GATE_TEMP = 16.0
GLA_CHUNK = 64
GLA_LEVELS = (64, 32, 16)
GLA_SUB = SUBLANES
GLA_UNROLL = 4

HY_ORDER = 2
HY_DIRS = 2
HY_BANDS = 8
HY_EMB = 1 + 2 * HY_BANDS
HY_FFN = 64
HY_FAST_DECAY = 0.3
HY_SLOW_DECAY = 1.5
HY_DECAY_TARGET = 1e-2
HY_PHASES = 4
HY_FTILE = 256
HY_CBLK = 256
HY_BBLK = 2
HY_RB = 32


def _cparams(*sem):
    return pltpu.CompilerParams(dimension_semantics=sem, vmem_limit_bytes=VMEM_LIMIT_BYTES)


def _const_spec(shape):
    nd = len(shape)
    return pl.BlockSpec(shape, lambda *_: (0,) * nd, pipeline_mode=pl.Buffered(1))


def _split_bf16(a):
    hi = a.astype(bf16)
    lo = (a - hi.astype(f32)).astype(bf16)
    return hi, lo


def _dot(a, b):
    return jnp.dot(a, b, preferred_element_type=f32)


def _dot_nt(a, b):
    return lax.dot_general(a, b, (((1,), (1,)), ((), ())), preferred_element_type=f32)


def _dot_tn(a, b):
    return lax.dot_general(a, b, (((0,), (0,)), ((), ())), preferred_element_type=f32)


def _dot3(a_hi, a_lo, b):
    b_hi, b_lo = _split_bf16(b)
    return _dot(a_hi, b_hi) + (_dot(a_lo, b_hi) + _dot(a_hi, b_lo))


def _dot2(a, w):
    a_hi, a_lo = _split_bf16(a)
    return _dot(a_hi, w) + _dot(a_lo, w)


def _rms(x):
    return x * lax.rsqrt(jnp.mean(x * x, axis=-1, keepdims=True) + NORM_EPS)


def _silu(x):
    return x / (1.0 + jnp.exp(-x))


def _dft_kernel(o_ref, *, n_fft, rows):
    half = n_fft // 2
    i = pl.program_id(0)
    n = lax.broadcasted_iota(jnp.int32, (rows, half), 0) + i * rows
    f = lax.broadcasted_iota(jnp.int32, (rows, half), 1)
    ang = ((n * f) & (n_fft - 1)).astype(f32) * (2.0 * math.pi / n_fft)
    re = jnp.cos(ang).astype(o_ref.dtype)
    nyq = jnp.where((n & 1) == 0, 1.0, -1.0)
    im = jnp.where(f == 0, nyq, -jnp.sin(ang)).astype(o_ref.dtype)
    ftile = o_ref.shape[2]
    n_ft = half // ftile
    for t in range(n_ft):
        o_ref[t] = re[:, t * ftile:(t + 1) * ftile]
        o_ref[n_ft + t] = im[:, t * ftile:(t + 1) * ftile]


def _dft_matrix(seq):
    rows = 256
    n_tiles = 2 * seq // HY_FTILE
    return pl.pallas_call(
        functools.partial(_dft_kernel, n_fft=2 * seq, rows=rows),
        grid=(seq // rows,),
        out_specs=pl.BlockSpec((n_tiles, rows, HY_FTILE), lambda i: (0, i, 0)),
        out_shape=jax.ShapeDtypeStruct((n_tiles, seq, HY_FTILE), bf16),
        compiler_params=_cparams("parallel"),
        name="dft_matrix",
    )()


def _filter_kernel(feats_ref, lag_ref, w1h_ref, w1l_ref, b1_ref, fr1_ref, w2h_ref, w2l_ref, b2_ref, fr2_ref,
                   w3fh_ref, w3fl_ref, w3bh_ref, w3bl_ref, delta_ref, skip_ref, ws_ref,
                   a_ref, b_ref, corr_ref, *, seq):
    nph = HY_PHASES
    ls = seq // nph
    ns = 2 * ls
    feats = feats_ref[...]
    hid = jnp.sin(fr1_ref[...] * (_dot3(w1h_ref[...], w1l_ref[...], feats) + b1_ref[...]))
    hid = jnp.sin(fr2_ref[...] * (_dot3(w2h_ref[...], w2l_ref[...], hid) + b2_ref[...]))
    fwd = _dot3(w3fh_ref[0], w3fl_ref[0], hid)
    bwd = _dot3(w3bh_ref[0], w3bl_ref[0], hid)
    lag = lag_ref[...]
    window = jnp.exp(-(lag * (1.0 / (seq - 1))) * delta_ref[...])
    fwd = fwd * window
    bwd = jnp.where(lag == 0.0, 0.0, bwd * window)
    mu = lax.broadcasted_iota(jnp.int32, (fwd.shape[0], ls), 1)
    first = mu == 0

    def phase(x, p):
        return x[:, p * ls:(p + 1) * ls]

    def delayed(x):
        return jnp.where(first, 0.0, pltpu.roll(x, 1, axis=1))

    skip = skip_ref[0]
    lane = lax.broadcasted_iota(jnp.int32, (fwd.shape[0], LANES), 1)
    n_ft = ws_ref.shape[0] // 2
    ftile = ws_ref.shape[2]
    ws_re = jnp.concatenate([ws_ref[t] for t in range(n_ft)], axis=1)
    ws_im = jnp.concatenate([ws_ref[n_ft + t] for t in range(n_ft)], axis=1)
    for d in range(-(nph - 1), nph):
        if d >= 0:
            pos = phase(fwd, d)
            neg = phase(bwd, 0) if d == 0 else delayed(phase(bwd, nph - d))
        else:
            pos = jnp.where(first, phase(bwd, -d), delayed(phase(fwd, nph + d)))
            neg = jnp.where(first, 0.0, phase(bwd, -d))
        sym = pos + neg
        g_re = _dot2(sym, ws_re)
        g_im = _dot2(pos - neg, ws_im)
        g_ny = _dot2(sym, ws_im[:, :LANES])[:, 0:1]
        if d == 0:
            g_re = g_re + skip
            g_ny = g_ny + skip
        a = g_re * jnp.where(first, 1.0 / ns, 2.0 / ns)
        b = jnp.where(first, 0.0, g_im * (2.0 / ns))
        for t in range(n_ft):
            a_ref[0, 0, t, d + nph - 1] = a[:, t * ftile:(t + 1) * ftile].astype(a_ref.dtype)
            b_ref[0, 0, t, d + nph - 1] = b[:, t * ftile:(t + 1) * ftile].astype(b_ref.dtype)
        corr = jnp.where(lane == 0, g_ny * (1.0 / ns) - a[:, 0:1], 0.0)
        corr_ref[0, 0, d + nph - 1] = corr.astype(corr_ref.dtype)


def _hyena_filters(feats, lag, w1, b1, fr1, w2, b2, fr2, w3, skip, ws, *, seq, width):
    pad = LANES
    nd = 2 * HY_PHASES - 1
    ls = seq // HY_PHASES

    def pad2(m, r, c):
        return jnp.zeros((r, c), f32).at[:m.shape[0], :m.shape[1]].set(m)

    w1h, w1l = _split_bf16(pad2(w1.T, pad, pad))
    w2h, w2l = _split_bf16(pad2(w2.T, pad, pad))
    w3t = pad2(w3.T, w3.shape[1], pad).reshape(HY_ORDER, HY_DIRS, width, pad)
    w3fh, w3fl = _split_bf16(w3t[:, 0])
    w3bh, w3bl = _split_bf16(w3t[:, 1])
    col = lambda v: pad2(v[:, None], pad, 1)
    max_decay = math.log(HY_DECAY_TARGET) / HY_FAST_DECAY
    min_decay = math.log(HY_DECAY_TARGET) / HY_SLOW_DECAY
    delta = jnp.abs(jnp.linspace(min_decay, max_decay, width, dtype=f32))[:, None]
    cb = HY_CBLK
    full = lambda shape: pl.BlockSpec(shape, lambda o, c: (0,) * len(shape))
    w3spec = pl.BlockSpec((1, cb, pad), lambda o, c: (o, c, 0))
    n_cb, n_ft = width // cb, ls // HY_FTILE
    out_spec = pl.BlockSpec((1, 1, n_ft, nd, cb, HY_FTILE), lambda o, c: (o, c, 0, 0, 0, 0))
    return pl.pallas_call(
        functools.partial(_filter_kernel, seq=seq),
        grid=(HY_ORDER, width // cb),
        in_specs=[full((pad, seq)), full((1, seq)),
                  full((pad, pad)), full((pad, pad)), full((pad, 1)), full((pad, 1)),
                  full((pad, pad)), full((pad, pad)), full((pad, 1)), full((pad, 1)),
                  w3spec, w3spec, w3spec, w3spec,
                  pl.BlockSpec((cb, 1), lambda o, c: (c, 0)),
                  pl.BlockSpec((1, cb, 1), lambda o, c: (o, c, 0)),
                  _const_spec(ws.shape)],
        out_specs=[out_spec, out_spec, pl.BlockSpec((1, 1, nd, cb, LANES), lambda o, c: (o, c, 0, 0, 0))],
        out_shape=[jax.ShapeDtypeStruct((HY_ORDER, n_cb, n_ft, nd, cb, HY_FTILE), bf16),
                   jax.ShapeDtypeStruct((HY_ORDER, n_cb, n_ft, nd, cb, HY_FTILE), bf16),
                   jax.ShapeDtypeStruct((HY_ORDER, n_cb, nd, cb, LANES), bf16)],
        compiler_params=_cparams("arbitrary", "arbitrary"),
        name="hyena_filters",
    )(feats, lag, w1h, w1l, col(b1), col(fr1), w2h, w2l, col(b2), col(fr2),
      w3fh, w3fl, w3bh, w3bl, delta, skip[:, :, None], ws)


def _positional_features(seq):
    ls = seq // HY_PHASES
    col = np.arange(seq)
    lag = jnp.asarray(HY_PHASES * (col % ls) + col // ls, f32)[:, None]
    t = lag * (1.0 / (seq - 1))
    pos_w = (2.0 * math.pi / seq) * lag
    bands = jnp.linspace(1e-4, HY_BANDS - 1.0, HY_BANDS, dtype=f32)
    feats = jnp.concatenate([t, jnp.cos(bands * pos_w), -jnp.sin(bands * pos_w)], axis=-1)
    feats_t = jnp.zeros((LANES, seq), f32).at[:HY_EMB].set(feats.T)
    return feats_t, lag.T


def _in_proj_kernel(x_ref, nw_ref, wa_ref, wlr_ref, wt_ref, wg_ref, bg_ref,
                    q_ref, k_ref, v_ref, r_ref, gf_ref, gb_ref, x2_ref, vt_ref, hs_ref):
    nph = HY_PHASES
    tm = x_ref.shape[1]
    tmr = tm // nph
    hf = _rms(x_ref[0]) * nw_ref[...]
    h = hf.astype(bf16)
    z = _dot(_dot(h, wlr_ref[...]).astype(bf16), wg_ref[...]) + bg_ref[...]
    n_tok = wa_ref.shape[1] - x2_ref.shape[2] * x2_ref.shape[4]
    pa = _dot(h, wa_ref[:, :n_tok])
    g = (jnp.minimum(z, 0.0) - jnp.log(1.0 + jnp.exp(-jnp.abs(z)))) * (1.0 / GLA_GATE_TEMP)
    gf_ref[0] = g[:, :GLA_KW]
    gb_ref[0] = g[:, GLA_KW:]
    o = 0
    for ref in (q_ref, k_ref, v_ref, r_ref):
        w = ref.shape[-1]
        ref[0] = pa[:, o:o + w].astype(ref.dtype)
        o += w
    n_slabs = hs_ref.shape[0]
    for c in range(n_slabs):
        hs_ref[c] = hf[:, c * LANES:(c + 1) * LANES]
    hp = jnp.concatenate(
        [jnp.concatenate([hs_ref[c, pl.ds(r, tmr, stride=nph), :] for r in range(nph)], axis=0)
         for c in range(n_slabs)], axis=1).astype(bf16)
    x2 = _dot(hp, wa_ref[:, n_tok:]).astype(x2_ref.dtype)
    vt = _dot_nt(wt_ref[...], hp).astype(vt_ref.dtype)
    for r in range(nph):
        for ci in range(x2_ref.shape[2]):
            x2_ref[0, r, ci] = x2[r * tmr:(r + 1) * tmr, ci * HY_CBLK:(ci + 1) * HY_CBLK]
        vt_ref[0, r] = vt[:, r * tmr:(r + 1) * tmr]


def _in_proj(x, nw, wa, wlr, wt, wg, bg, *, tm):
    bsz, seq, d = x.shape
    nph = HY_PHASES
    hw = wt.shape[0] // 2
    sr, tmr = seq // nph, tm // nph
    row = lambda w: pl.BlockSpec((1, tm, w), lambda b, t: (b, t, 0))
    tok = lambda w, dt: jax.ShapeDtypeStruct((bsz, seq, w), dt)
    return pl.pallas_call(
        _in_proj_kernel,
        grid=(bsz, seq // tm),
        in_specs=[row(d)] + [_const_spec(w.shape) for w in (nw, wa, wlr, wt, wg, bg)],
        out_specs=[row(GLA_KW), row(GLA_KW), row(GLA_WIDTH), row(GLA_WIDTH), row(GLA_KW), row(GLA_KW),
                   pl.BlockSpec((1, nph, hw // HY_CBLK, tmr, HY_CBLK), lambda b, t: (b, 0, 0, t, 0)),
                   pl.BlockSpec((1, nph, 2 * hw, tmr), lambda b, t: (b, 0, 0, t))],
        out_shape=[tok(GLA_KW, bf16), tok(GLA_KW, bf16), tok(GLA_WIDTH, bf16), tok(GLA_WIDTH, bf16),
                   tok(GLA_KW, f32), tok(GLA_KW, f32),
                   jax.ShapeDtypeStruct((bsz, nph, hw // HY_CBLK, sr, HY_CBLK), bf16),
                   jax.ShapeDtypeStruct((bsz, nph, 2 * hw, sr), bf16)],
        scratch_shapes=[pltpu.VMEM((d // LANES, tm, LANES), f32)],
        compiler_params=_cparams("parallel", "parallel"),
        name="in_proj",
    )(x, nw, wa, wlr, wt, wg, bg)


def _block_diag(n_blocks, nr, nc):
    rr = np.arange(n_blocks * nr)[:, None] // nr
    cc = np.arange(n_blocks * nc)[None, :] // nc
    return (rr == cc).astype(np.float32)


def _gla_constants(rev):
    c, s, h = GLA_CHUNK, GLA_SUB, GLA_HEADS
    i = np.arange(c)[:, None]
    j = np.arange(c)[None, :]
    tri = (j >= i) if rev else (j <= i)
    lane_j = np.arange(h * c)[None, :] % c
    query_rows = np.zeros((c, len(GLA_LEVELS)), np.float32)
    same = np.zeros((len(GLA_LEVELS), c, h * c), np.float32)
    for l, size in enumerate(GLA_LEVELS):
        upper = (i % size) >= size // 2
        query_rows[:, l:l + 1] = ~upper if rev else upper
        same[l] = (i // size) == (lane_j // size)
    lag = np.zeros((s, c, h * c), np.float32)
    for d in range(s):
        if rev:
            lag[d] = (lane_j == i + d) & (i % s + d <= s - 1)
        else:
            lag[d] = (lane_j == i - d) & (i % s >= d)
    return tri.astype(np.float32), query_rows, same[1:], lag


def _interleave(*stages):
    live = list(stages)
    while live:
        for gen in list(live):
            try:
                next(gen)
            except StopIteration:
                live.remove(gen)


def _gla_chunk(t0, rev, q_ref, k_ref, v_ref, g_ref, st_ref, o_ref, p_ref, sin_ref,
               tri_ref, rows_ref, same_ref, lag_ref, bd2_ref, bd4_ref, ones_bd_ref):
    c, s, nh = GLA_CHUNK, GLA_SUB, GLA_HEADS
    kw = GLA_KW
    q = q_ref[0, pl.ds(t0, c), :].astype(f32)
    k = k_ref[0, pl.ds(t0, c), :].astype(f32)
    v = v_ref[0, pl.ds(t0, c), :]
    g = g_ref[0, pl.ds(t0, c), :]
    g_hi, g_lo = _split_bf16(g)
    tri = tri_ref[...]
    b = _dot(tri, g_hi) + _dot(tri, g_lo)
    yield
    b_end = b[0:1] if rev else b[c - 1:c]

    def stack_heads(x):
        return jnp.concatenate([x] * nh, axis=0)

    st = st_ref[...]
    sin_ref[...] = st.astype(bf16)
    bd4 = bd4_ref[...]
    vst = jnp.concatenate([v[:, h * GLA_DV:(h + 1) * GLA_DV] for h in range(nh)], axis=0)
    kt_bd = stack_heads((k * jnp.exp(b_end - b)).astype(bf16)) * bd4
    st_ref[...] = st * jnp.exp(b_end) + _dot_tn(vst, kt_bd)
    yield

    a = None
    for l, size in enumerate(GLA_LEVELS):
        half = size // 2
        ref_rows = [blk * size + (half if rev else half - 1) for blk in range(c // size)]
        ref = jnp.concatenate([jnp.broadcast_to(b[r:r + 1, :], (size, kw)) for r in ref_rows], axis=0)
        is_q = rows_ref[:, l:l + 1] > 0.5
        e = jnp.exp(jnp.minimum(jnp.where(is_q, b - ref, ref - b), 0.0))
        zero = jnp.zeros_like(q)
        ql = jnp.where(is_q, q * e, zero).astype(bf16)
        kl = jnp.where(is_q, zero, k * e).astype(bf16)
        parts = []
        for p in range(kw // LANES):
            sl = slice(p * LANES, (p + 1) * LANES)
            kbd = jnp.concatenate([kl[:, sl]] * 2, axis=0) * bd2_ref[...]
            parts.append(_dot_nt(ql[:, sl], kbd))
        al = jnp.concatenate(parts, axis=1)
        if l > 0:
            al = al * same_ref[l - 1]
        a = al if a is None else a + al
        yield

    gam = jnp.exp(g)
    x = k
    for d in range(s):
        if d > 0:
            x3 = x.reshape(c // s, s, kw)
            x = pltpu.roll(x3, (s - 1) if rev else 1, axis=1).reshape(c, kw) * gam
        p_ref[d * c:(d + 1) * c, :] = (q * x).astype(bf16)
    red = _dot(p_ref[...], ones_bd_ref[...])
    yield
    for d in range(s):
        a = a + red[d * c:(d + 1) * c, :] * lag_ref[d]

    a_st = stack_heads(a.astype(bf16)) * bd4
    qs_st = stack_heads((q * jnp.exp(b)).astype(bf16)) * bd4
    o_st = _dot(a_st, vst) + _dot_nt(qs_st, sin_ref[...])
    yield
    for h in range(nh):
        o_ref[pl.ds(t0, c), h * GLA_DV:(h + 1) * GLA_DV] = o_st[h * c:(h + 1) * c, :]


def _gla_kernel(q_ref, k_ref, v_ref, gf_ref, gb_ref,
                trif_ref, rowsf_ref, samef_ref, lagf_ref, trib_ref, rowsb_ref, sameb_ref, lagb_ref,
                bd2_ref, bd4_ref, ones_bd_ref,
                out_ref,
                of_ref, ob_ref, stf_ref, stb_ref, p_ref, sin_ref, *, seq, rows):
    c = GLA_CHUNK
    stf_ref[...] = jnp.zeros_like(stf_ref)
    stb_ref[...] = jnp.zeros_like(stb_ref)

    consts = (bd2_ref, bd4_ref, ones_bd_ref)
    n_chunks = seq // c

    def scan(i, carry):
        chains = []
        for u in range(GLA_UNROLL):
            n = i * GLA_UNROLL + u
            tf = pl.multiple_of(n * c, c)
            tb = pl.multiple_of((n_chunks - 1 - n) * c, c)
            chains.append(_gla_chunk(tf, False, q_ref, k_ref, v_ref, gf_ref, stf_ref, of_ref,
                                     p_ref.at[2 * u], sin_ref.at[2 * u],
                                     trif_ref, rowsf_ref, samef_ref, lagf_ref, *consts))
            chains.append(_gla_chunk(tb, True, q_ref, k_ref, v_ref, gb_ref, stb_ref, ob_ref,
                                     p_ref.at[2 * u + 1], sin_ref.at[2 * u + 1],
                                     trib_ref, rowsb_ref, sameb_ref, lagb_ref, *consts))
        _interleave(*chains)
        return carry

    lax.fori_loop(0, n_chunks // GLA_UNROLL, scan, 0)

    def finish(i, carry):
        r0 = pl.multiple_of(i * rows, rows)
        o = of_ref[pl.ds(r0, rows), :] + ob_ref[pl.ds(r0, rows), :]
        out_ref[0, pl.ds(r0, rows), :] = o.astype(out_ref.dtype)
        return carry

    lax.fori_loop(0, seq // rows, finish, 0)


def _gla(q, k, v, gf, gb):
    bsz, seq, _ = q.shape
    c, s, h = GLA_CHUNK, GLA_SUB, GLA_HEADS
    consts = []
    for rev in (False, True):
        tri, rows, same, lag = _gla_constants(rev)
        consts += [jnp.asarray(tri, bf16), jnp.asarray(rows), jnp.asarray(same), jnp.asarray(lag)]
    consts += [jnp.asarray(_block_diag(2, c, GLA_DK), bf16),
               jnp.asarray(_block_diag(h, c, GLA_DK), bf16),
               jnp.asarray(_block_diag(h, GLA_DK, c), bf16)]
    per_b = lambda w: pl.BlockSpec((1, seq, w), lambda b: (b, 0, 0))
    return pl.pallas_call(
        functools.partial(_gla_kernel, seq=seq, rows=256),
        grid=(bsz,),
        in_specs=[per_b(GLA_KW), per_b(GLA_KW), per_b(GLA_WIDTH), per_b(GLA_KW), per_b(GLA_KW)]
        + [_const_spec(x.shape) for x in consts],
        out_specs=per_b(GLA_WIDTH),
        out_shape=jax.ShapeDtypeStruct((bsz, seq, GLA_WIDTH), bf16),
        scratch_shapes=[pltpu.VMEM((seq, GLA_WIDTH), f32), pltpu.VMEM((seq, GLA_WIDTH), f32),
                        pltpu.VMEM((GLA_DV, GLA_KW), f32), pltpu.VMEM((GLA_DV, GLA_KW), f32),
                        pltpu.VMEM((2 * GLA_UNROLL, s * c, GLA_KW), bf16),
                        pltpu.VMEM((2 * GLA_UNROLL, GLA_DV, GLA_KW), bf16)],
        compiler_params=_cparams("parallel"),
        name="gla",
    )(q, k, v, gf, gb, *consts)


def _short_conv_phases(u, w, axis):
    nph = len(u)
    n = u[0].shape[axis]
    m = lax.broadcasted_iota(jnp.int32, u[0].shape, axis)
    before_first = jnp.where(m == 0, 0.0, pltpu.roll(u[nph - 1], 1, axis=axis))
    after_last = jnp.where(m == n - 1, 0.0, pltpu.roll(u[0], n - 1, axis=axis))
    out = []
    for r in range(nph):
        prev = u[r - 1] if r > 0 else before_first
        nxt = u[r + 1] if r < nph - 1 else after_last
        out.append(prev * w[0] + u[r] * w[1] + nxt * w[2] + w[3])
    return out


def _hyena_kernel(v_ref, x1_ref, x2_ref, cwt_v_ref, cwt_x1_ref, cw_x2_ref, a_ref, b_ref, corr_ref,
                  wre_ref, wim_ref, out_ref, src_ref, acc1_ref, acc2_ref, u_ref, y_ref, tok_ref):
    order = pl.program_id(2)
    ft = pl.program_id(3)
    n_ft = pl.num_programs(3)
    nb, nph, cb, ls = v_ref.shape
    rows = nb * cb

    rb = HY_RB

    def conv_to_src(u_ref_in, w_ref, gate_ref):
        def body(i, carry):
            bi = i // (cb // rb)
            crow = pl.multiple_of((i % (cb // rb)) * rb, rb)
            row0 = pl.multiple_of(i * rb, rb)
            w = w_ref[pl.ds(crow, rb), :]
            u = [u_ref_in[bi, r, pl.ds(crow, rb), :].astype(f32) for r in range(nph)]
            c = _short_conv_phases(u, [w[:, j:j + 1] for j in range(4)], axis=1)
            for r in range(nph):
                val = c[r] if gate_ref is None else c[r] * gate_ref[r, pl.ds(row0, rb), :]
                src_ref[r, pl.ds(row0, rb), :] = val.astype(bf16)
            return carry
        lax.fori_loop(0, rows // rb, body, 0, unroll=4)

    @pl.when(jnp.logical_and(order == 0, ft == 0))
    def _():
        conv_to_src(v_ref, cwt_v_ref, None)
        acc1_ref[...] = jnp.zeros_like(acc1_ref)

    @pl.when(jnp.logical_and(order == 1, ft == 0))
    def _():
        conv_to_src(x1_ref, cwt_x1_ref, acc1_ref)
        acc2_ref[...] = jnp.zeros_like(acc2_ref)

    wre = wre_ref[0]
    wim = wim_ref[0]
    ftile = wre.shape[1]
    for r in range(nph):
        u_ref[2 * r] = _dot(src_ref[r], wre).astype(bf16)
        u_ref[2 * r + 1] = _dot(src_ref[r], wim).astype(bf16)
    first_tile = (ft == 0).astype(f32).astype(bf16)

    rb = HY_RB

    def combine(i, carry):
        row0 = pl.multiple_of(i * rb, rb)
        crow = pl.multiple_of((i % (cb // rb)) * rb, rb)
        for lane0 in range(0, ftile, LANES):
            cols = slice(lane0, lane0 + LANES)
            u_re = [u_ref[2 * r, pl.ds(row0, rb), cols] for r in range(nph)]
            u_im = [u_ref[2 * r + 1, pl.ds(row0, rb), cols] for r in range(nph)]
            for rp in range(nph):
                y_re = y_im = None
                for r in range(nph):
                    d = rp - r + nph - 1
                    a = a_ref[0, 0, 0, d, pl.ds(crow, rb), cols]
                    b = b_ref[0, 0, 0, d, pl.ds(crow, rb), cols]
                    t_re = u_re[r] * a - u_im[r] * b
                    t_im = u_re[r] * b + u_im[r] * a
                    if lane0 == 0:
                        t_im = t_im + u_im[r] * (corr_ref[0, 0, d, pl.ds(crow, rb), :] * first_tile)
                    y_re = t_re if y_re is None else y_re + t_re
                    y_im = t_im if y_im is None else y_im + t_im
                y_ref[2 * rp, pl.ds(row0, rb), cols] = y_re
                y_ref[2 * rp + 1, pl.ds(row0, rb), cols] = y_im
        return carry

    lax.fori_loop(0, rows // rb, combine, 0)

    @pl.when(order == 0)
    def _():
        for rp in range(nph):
            acc1_ref[rp] += _dot_nt(y_ref[2 * rp], wre) + _dot_nt(y_ref[2 * rp + 1], wim)

    @pl.when(order == 1)
    def _():
        for rp in range(nph):
            acc2_ref[rp] += _dot_nt(wre, y_ref[2 * rp]) + _dot_nt(wim, y_ref[2 * rp + 1])

    @pl.when(jnp.logical_and(order == 1, ft == n_ft - 1))
    def _():
        w = cw_x2_ref[...]
        taps = [w[i:i + 1] for i in range(4)]
        for i in range(nb):
            x2 = _short_conv_phases([x2_ref[i, r, 0].astype(f32) for r in range(nph)], taps, axis=0)
            for r in range(nph):
                y = x2[r] * acc2_ref[r][:, i * cb:(i + 1) * cb]
                for c in range(cb // LANES):
                    tok_ref[c, pl.ds(r, ls, stride=nph), :] = y[:, c * LANES:(c + 1) * LANES]
            for c in range(cb // LANES):
                out_ref[i, 0, :, c * LANES:(c + 1) * LANES] = tok_ref[c].astype(out_ref.dtype)


def _hyena(vx1t, x2, cwt, cw_x2, fa, fb, fcorr, ws):
    bsz, nph, two_w, ls = vx1t.shape
    width = two_w // 2
    cb, nb, ftile = HY_CBLK, HY_BBLK, HY_FTILE
    n_cb = width // cb
    n_ft = ls // ftile
    nd = fa.shape[3]
    rows = nb * cb
    filt = lambda w: pl.BlockSpec((1, 1, 1, nd, cb, w), lambda c, b, o, f: (o, c, f, 0, 0, 0))
    return pl.pallas_call(
        _hyena_kernel,
        grid=(n_cb, bsz // nb, HY_ORDER, n_ft),
        in_specs=[pl.BlockSpec((nb, nph, cb, ls), lambda c, b, o, f: (b, 0, c, 0)),
                  pl.BlockSpec((nb, nph, cb, ls), lambda c, b, o, f: (b, 0, n_cb + c, 0)),
                  pl.BlockSpec((nb, nph, 1, ls, cb), lambda c, b, o, f: (b, 0, c, 0, 0)),
                  pl.BlockSpec((cb, 4), lambda c, b, o, f: (c, 0)),
                  pl.BlockSpec((cb, 4), lambda c, b, o, f: (n_cb + c, 0)),
                  pl.BlockSpec((4, cb), lambda c, b, o, f: (0, c)),
                  filt(ftile), filt(ftile),
                  pl.BlockSpec((1, 1, nd, cb, LANES), lambda c, b, o, f: (o, c, 0, 0, 0)),
                  pl.BlockSpec((1, ls, ftile), lambda c, b, o, f: (f, 0, 0)),
                  pl.BlockSpec((1, ls, ftile), lambda c, b, o, f: (n_ft + f, 0, 0))],
        out_specs=pl.BlockSpec((nb, 1, nph * ls, cb), lambda c, b, o, f: (b, c, 0, 0)),
        out_shape=jax.ShapeDtypeStruct((bsz, n_cb, nph * ls, cb), bf16),
        scratch_shapes=[pltpu.VMEM((nph, rows, ls), bf16), pltpu.VMEM((nph, rows, ls), f32),
                        pltpu.VMEM((nph, ls, rows), f32),
                        pltpu.VMEM((2 * nph, rows, ftile), bf16), pltpu.VMEM((2 * nph, rows, ftile), bf16),
                        pltpu.VMEM((cb // LANES, nph * ls, LANES), f32)],
        compiler_params=_cparams("parallel", "parallel", "arbitrary", "arbitrary"),
        name="hyena",
    )(vx1t, vx1t, x2, cwt, cwt, cw_x2, fa, fb, fcorr, ws, ws)


def _out_ffn_kernel(x_ref, og_ref, r_ref, hy_ref, gnw_ref, hnw_ref, wo_g_ref, wo_h_ref, fnw_ref, wg_ref, wu_ref,
                    wd_ref, *rest, final_norm):
    if final_norm:
        lnw_ref, out_ref = rest
    else:
        (out_ref,) = rest
    o = og_ref[...].astype(f32)
    gate = _silu(r_ref[...].astype(f32))
    gnw = gnw_ref[...]
    og = jnp.concatenate([_rms(o[:, h * GLA_DV:(h + 1) * GLA_DV]) * gnw for h in range(GLA_HEADS)], axis=1)
    og = (og * gate).astype(bf16)
    hy = jnp.concatenate([hy_ref[0, ci] for ci in range(hy_ref.shape[1])], axis=1)
    hy = (_rms(hy.astype(f32)) * hnw_ref[...]).astype(bf16)
    x = x_ref[...] + _dot(og, wo_g_ref[...]) + _dot(hy, wo_h_ref[...])
    h = (_rms(x) * fnw_ref[...]).astype(bf16)
    ff = (_silu(_dot(h, wg_ref[...])) * _dot(h, wu_ref[...])).astype(bf16)
    x = x + _dot(ff, wd_ref[...])
    if final_norm:
        x = _rms(x) * lnw_ref[...]
    out_ref[...] = x


def _out_ffn(x, og, r, hy, gnw, hnw, wo_g, wo_h, fnw, wg, wu, wd, lnw, *, tm):
    n, d = x.shape
    _, n_cb, seq, cb = hy.shape
    tiles_per_b = seq // tm
    row = lambda w: pl.BlockSpec((tm, w), lambda i: (i, 0))
    weights = [gnw, hnw, wo_g, wo_h, fnw, wg, wu, wd] + ([] if lnw is None else [lnw])
    return pl.pallas_call(
        functools.partial(_out_ffn_kernel, final_norm=lnw is not None),
        grid=(n // tm,),
        in_specs=[row(d), row(og.shape[1]), row(r.shape[1]),
                  pl.BlockSpec((1, n_cb, tm, cb), lambda i: (i // tiles_per_b, 0, i % tiles_per_b, 0))]
        + [_const_spec(w.shape) for w in weights],
        out_specs=row(d),
        out_shape=jax.ShapeDtypeStruct((n, d), f32),
        compiler_params=_cparams("parallel"),
        name="out_ffn",
    )(x, og, r, hy, *weights)


def _prepare_layer(norm_mix, w_in, wg_f, bg_f, wg_b, bg_b, gla_norm, conv_w, conv_b, hy_norm, w_out,
                   norm_ffn, w_gate, w_up, w_down):
    d = w_in.shape[0]
    hw = hy_norm.shape[0]
    o_lr = 2 * GLA_KW + 2 * GLA_WIDTH
    o_hy = o_lr + 2 * GLA_GATE_RANK
    wq = w_in[:, :GLA_KW] * (GLA_DK ** -0.5)
    wa = jnp.concatenate([wq, w_in[:, GLA_KW:o_lr], w_in[:, o_hy + 2 * hw:]], axis=1).astype(bf16)
    wlr = jnp.zeros((d, LANES), f32).at[:, :2 * GLA_GATE_RANK].set(w_in[:, o_lr:o_hy]).astype(bf16)
    wt = w_in[:, o_hy:o_hy + 2 * hw].T.astype(bf16)
    wg = jnp.zeros((LANES, 2 * GLA_KW), f32)
    wg = wg.at[:GLA_GATE_RANK, :GLA_KW].set(wg_f).at[GLA_GATE_RANK:2 * GLA_GATE_RANK, GLA_KW:].set(wg_b)
    wg = wg.astype(bf16)
    bg = jnp.concatenate([bg_f, bg_b])[None, :]
    cw = jnp.concatenate([conv_w, conv_b[None, :]], axis=0)
    return dict(
        nw=norm_mix[None, :], wa=wa, wlr=wlr, wt=wt, wgate=wg, bg=bg, gnw=gla_norm[None, :],
        cwt=cw[:, :2 * hw].T, cw_x2=cw[:, 2 * hw:], hnw=hy_norm[None, :],
        wo_g=w_out[:GLA_WIDTH].astype(bf16), wo_h=w_out[GLA_WIDTH:].astype(bf16),
        fnw=norm_ffn[None, :], wg=w_gate.astype(bf16), wu=w_up.astype(bf16), wd=w_down.astype(bf16))


def _layer(x, p, filt, wf, lnw):
    q, k, v, r, gf, gb, x2, vx1t = _in_proj(x, p["nw"], p["wa"], p["wlr"], p["wt"], p["wgate"], p["bg"],
                                            tm=512)
    og = _gla(q, k, v, gf, gb)
    hy = _hyena(vx1t, x2, p["cwt"], p["cw_x2"], *filt, wf)
    bsz, seq, d = x.shape
    n = bsz * seq
    y = _out_ffn(x.reshape(n, d), og.reshape(n, -1), r.reshape(n, -1), hy, p["gnw"], p["hnw"],
                 p["wo_g"], p["wo_h"], p["fnw"], p["wg"], p["wu"], p["wd"], lnw, tm=512)
    return y.reshape(bsz, seq, d)


def kernel(x_prompt, x_sample, norm_mix, w_in, gla_wg_f, gla_bg_f, gla_wg_b, gla_bg_b, gla_norm, hy_conv_w, hy_conv_b, hy_w1, hy_b1, hy_freq1, hy_w2, hy_b2, hy_freq2, hy_w3, hy_skip, hy_norm, w_out, norm_ffn, w_gate, w_up, w_down, norm_final):
    depth = w_in.shape[0]
    seq = x_prompt.shape[1]
    assert x_sample.shape[1] == seq
    width = hy_norm.shape[1]
    wf = _dft_matrix(seq // HY_PHASES)
    feats, lag = _positional_features(seq)
    layers, filters = [], []
    for l in range(depth):
        layers.append(_prepare_layer(norm_mix[l], w_in[l], gla_wg_f[l], gla_bg_f[l], gla_wg_b[l], gla_bg_b[l],
                                     gla_norm[l], hy_conv_w[l], hy_conv_b[l], hy_norm[l], w_out[l],
                                     norm_ffn[l], w_gate[l], w_up[l], w_down[l]))
        filters.append(_hyena_filters(feats, lag, hy_w1[l], hy_b1[l], hy_freq1[l], hy_w2[l], hy_b2[l], hy_freq2[l],
                                      hy_w3[l], hy_skip[l], wf, seq=seq, width=width))
    outs = []
    for x in (x_prompt, x_sample):
        for l in range(depth):
            x = _layer(x, layers[l], filters[l], wf, norm_final[None, :] if l == depth - 1 else None)
        outs.append(x)
    return tuple(outs)
```

```python
import functools
import math

import numpy as np
import jax
import jax.numpy as jnp
from jax import lax
from jax.experimental import pallas as pl
from jax.experimental.pallas import tpu as pltpu

f32 = jnp.float32
bf16 = jnp.bfloat16

LANES = 128
SUBLANES = 8
VMEM_LIMIT_BYTES = 56 * 1024 * 1024

NORM_EPS = 1e-6

GLA_HEADS = 4
GLA_DK = 64
GLA_DV = 128
GLA_KW = GLA_HEADS * GLA_DK
GLA_WIDTH = GLA_HEADS * GLA_DV
GLA_GATE_RANK = 16
GLA_GATE_TEMP = 16.0
GLA_CHUNK = 64
GLA_LEVELS = (64, 32, 16)
GLA_SUB = SUBLANES
GLA_UNROLL = 4

HY_ORDER = 2
HY_DIRS = 2
HY_BANDS = 8
HY_EMB = 1 + 2 * HY_BANDS
HY_FFN = 64
HY_FAST_DECAY = 0.3
HY_SLOW_DECAY = 1.5
HY_DECAY_TARGET = 1e-2
HY_PHASES = 4
HY_FTILE = 512
HY_CBLK = 256
HY_BBLK = 2
HY_RB = 32


def _cparams(*sem):
    return pltpu.CompilerParams(dimension_semantics=sem, vmem_limit_bytes=VMEM_LIMIT_BYTES)


def _const_spec(shape):
    nd = len(shape)
    return pl.BlockSpec(shape, lambda *_: (0,) * nd, pipeline_mode=pl.Buffered(1))


def _split_bf16(a):
    hi = a.astype(bf16)
    lo = (a - hi.astype(f32)).astype(bf16)
    return hi, lo


def _dot(a, b):
    return jnp.dot(a, b, preferred_element_type=f32)


def _dot_nt(a, b):
    return lax.dot_general(a, b, (((1,), (1,)), ((), ())), preferred_element_type=f32)


def _dot_tn(a, b):
    return lax.dot_general(a, b, (((0,), (0,)), ((), ())), preferred_element_type=f32)


def _dot3(a_hi, a_lo, b):
    b_hi, b_lo = _split_bf16(b)
    return _dot(a_hi, b_hi) + (_dot(a_lo, b_hi) + _dot(a_hi, b_lo))


def _dot2(a, w):
    a_hi, a_lo = _split_bf16(a)
    return _dot(a_hi, w) + _dot(a_lo, w)


def _rms(x):
    return x * lax.rsqrt(jnp.mean(x * x, axis=-1, keepdims=True) + NORM_EPS)


def _silu(x):
    return x / (1.0 + jnp.exp(-x))


def _dft_kernel(o_ref, *, n_fft, rows):
    half = n_fft // 2
    i = pl.program_id(0)
    n = lax.broadcasted_iota(jnp.int32, (rows, half), 0) + i * rows
    f = lax.broadcasted_iota(jnp.int32, (rows, half), 1)
    ang = ((n * f) & (n_fft - 1)).astype(f32) * (2.0 * math.pi / n_fft)
    re = jnp.cos(ang).astype(o_ref.dtype)
    nyq = jnp.where((n & 1) == 0, 1.0, -1.0)
    im = jnp.where(f == 0, nyq, -jnp.sin(ang)).astype(o_ref.dtype)
    ftile = o_ref.shape[2]
    n_ft = half // ftile
    for t in range(n_ft):
        o_ref[t] = re[:, t * ftile:(t + 1) * ftile]
        o_ref[n_ft + t] = im[:, t * ftile:(t + 1) * ftile]


def _dft_matrix(seq):
    rows = 256
    n_tiles = 2 * seq // HY_FTILE
    return pl.pallas_call(
        functools.partial(_dft_kernel, n_fft=2 * seq, rows=rows),
        grid=(seq // rows,),
        out_specs=pl.BlockSpec((n_tiles, rows, HY_FTILE), lambda i: (0, i, 0)),
        out_shape=jax.ShapeDtypeStruct((n_tiles, seq, HY_FTILE), bf16),
        compiler_params=_cparams("parallel"),
        name="dft_matrix",
    )()


def _filter_kernel(feats_ref, lag_ref, w1h_ref, w1l_ref, b1_ref, fr1_ref, w2h_ref, w2l_ref, b2_ref, fr2_ref,
                   w3fh_ref, w3fl_ref, w3bh_ref, w3bl_ref, delta_ref, skip_ref, ws_ref,
                   a_ref, b_ref, corr_ref, *, seq):
    nph = HY_PHASES
    ls = seq // nph
    ns = 2 * ls
    feats = feats_ref[...]
    hid = jnp.sin(fr1_ref[...] * (_dot3(w1h_ref[...], w1l_ref[...], feats) + b1_ref[...]))
    hid = jnp.sin(fr2_ref[...] * (_dot3(w2h_ref[...], w2l_ref[...], hid) + b2_ref[...]))
    fwd = _dot3(w3fh_ref[0], w3fl_ref[0], hid)
    bwd = _dot3(w3bh_ref[0], w3bl_ref[0], hid)
    lag = lag_ref[...]
    window = jnp.exp(-(lag * (1.0 / (seq - 1))) * delta_ref[...])
    fwd = fwd * window
    bwd = jnp.where(lag == 0.0, 0.0, bwd * window)
    mu = lax.broadcasted_iota(jnp.int32, (fwd.shape[0], ls), 1)
    first = mu == 0

    def phase(x, p):
        return x[:, p * ls:(p + 1) * ls]

    def delayed(x):
        return jnp.where(first, 0.0, pltpu.roll(x, 1, axis=1))

    skip = skip_ref[0]
    lane = lax.broadcasted_iota(jnp.int32, (fwd.shape[0], LANES), 1)
    n_ft = ws_ref.shape[0] // 2
    ftile = ws_ref.shape[2]
    ws_re = jnp.concatenate([ws_ref[t] for t in range(n_ft)], axis=1)
    ws_im = jnp.concatenate([ws_ref[n_ft + t] for t in range(n_ft)], axis=1)
    for d in range(-(nph - 1), nph):
        if d >= 0:
            pos = phase(fwd, d)
            neg = phase(bwd, 0) if d == 0 else delayed(phase(bwd, nph - d))
        else:
            pos = jnp.where(first, phase(bwd, -d), delayed(phase(fwd, nph + d)))
            neg = jnp.where(first, 0.0, phase(bwd, -d))
        sym = pos + neg
        g_re = _dot2(sym, ws_re)
        g_im = _dot2(pos - neg, ws_im)
        g_ny = _dot2(sym, ws_im[:, :LANES])[:, 0:1]
        if d == 0:
            g_re = g_re + skip
            g_ny = g_ny + skip
        a = g_re * jnp.where(first, 1.0 / ns, 2.0 / ns)
        b = jnp.where(first, 0.0, g_im * (2.0 / ns))
        for t in range(n_ft):
            a_ref[0, 0, t, d + nph - 1] = a[:, t * ftile:(t + 1) * ftile].astype(a_ref.dtype)
            b_ref[0, 0, t, d + nph - 1] = b[:, t * ftile:(t + 1) * ftile].astype(b_ref.dtype)
        corr = jnp.where(lane == 0, g_ny * (1.0 / ns) - a[:, 0:1], 0.0)
        corr_ref[0, 0, d + nph - 1] = corr.astype(corr_ref.dtype)


def _hyena_filters(feats, lag, w1, b1, fr1, w2, b2, fr2, w3, skip, ws, *, seq, width):
    pad = LANES
    nd = 2 * HY_PHASES - 1
    ls = seq // HY_PHASES

    def pad2(m, r, c):
        return jnp.zeros((r, c), f32).at[:m.shape[0], :m.shape[1]].set(m)

    w1h, w1l = _split_bf16(pad2(w1.T, pad, pad))
    w2h, w2l = _split_bf16(pad2(w2.T, pad, pad))
    w3t = pad2(w3.T, w3.shape[1], pad).reshape(HY_ORDER, HY_DIRS, width, pad)
    w3fh, w3fl = _split_bf16(w3t[:, 0])
    w3bh, w3bl = _split_bf16(w3t[:, 1])
    col = lambda v: pad2(v[:, None], pad, 1)
    max_decay = math.log(HY_DECAY_TARGET) / HY_FAST_DECAY
    min_decay = math.log(HY_DECAY_TARGET) / HY_SLOW_DECAY
    delta = jnp.abs(jnp.linspace(min_decay, max_decay, width, dtype=f32))[:, None]
    cb = HY_CBLK
    full = lambda shape: pl.BlockSpec(shape, lambda o, c: (0,) * len(shape))
    w3spec = pl.BlockSpec((1, cb, pad), lambda o, c: (o, c, 0))
    n_cb, n_ft = width // cb, ls // HY_FTILE
    out_spec = pl.BlockSpec((1, 1, n_ft, nd, cb, HY_FTILE), lambda o, c: (o, c, 0, 0, 0, 0))
    return pl.pallas_call(
        functools.partial(_filter_kernel, seq=seq),
        grid=(HY_ORDER, width // cb),
        in_specs=[full((pad, seq)), full((1, seq)),
                  full((pad, pad)), full((pad, pad)), full((pad, 1)), full((pad, 1)),
                  full((pad, pad)), full((pad, pad)), full((pad, 1)), full((pad, 1)),
                  w3spec, w3spec, w3spec, w3spec,
                  pl.BlockSpec((cb, 1), lambda o, c: (c, 0)),
                  pl.BlockSpec((1, cb, 1), lambda o, c: (o, c, 0)),
                  _const_spec(ws.shape)],
        out_specs=[out_spec, out_spec, pl.BlockSpec((1, 1, nd, cb, LANES), lambda o, c: (o, c, 0, 0, 0))],
        out_shape=[jax.ShapeDtypeStruct((HY_ORDER, n_cb, n_ft, nd, cb, HY_FTILE), bf16),
                   jax.ShapeDtypeStruct((HY_ORDER, n_cb, n_ft, nd, cb, HY_FTILE), bf16),
                   jax.ShapeDtypeStruct((HY_ORDER, n_cb, nd, cb, LANES), bf16)],
        compiler_params=_cparams("arbitrary", "arbitrary"),
        name="hyena_filters",
    )(feats, lag, w1h, w1l, col(b1), col(fr1), w2h, w2l, col(b2), col(fr2),
      w3fh, w3fl, w3bh, w3bl, delta, skip[:, :, None], ws)


def _positional_features(seq):
    ls = seq // HY_PHASES
    col = np.arange(seq)
    lag = jnp.asarray(HY_PHASES * (col % ls) + col // ls, f32)[:, None]
    t = lag * (1.0 / (seq - 1))
    pos_w = (2.0 * math.pi / seq) * lag
    bands = jnp.linspace(1e-4, HY_BANDS - 1.0, HY_BANDS, dtype=f32)
    feats = jnp.concatenate([t, jnp.cos(bands * pos_w), -jnp.sin(bands * pos_w)], axis=-1)
    feats_t = jnp.zeros((LANES, seq), f32).at[:HY_EMB].set(feats.T)
    return feats_t, lag.T


def _in_proj_kernel(x_ref, nw_ref, wa_ref, wlr_ref, wt_ref, wg_ref, bg_ref,
                    q_ref, k_ref, v_ref, r_ref, gf_ref, gb_ref, x2_ref, vt_ref, hs_ref):
    nph = HY_PHASES
    tm = x_ref.shape[1]
    tmr = tm // nph
    hf = _rms(x_ref[0]) * nw_ref[...]
    h = hf.astype(bf16)
    z = _dot(_dot(h, wlr_ref[...]).astype(bf16), wg_ref[...]) + bg_ref[...]
    n_tok = wa_ref.shape[1] - x2_ref.shape[2] * x2_ref.shape[4]
    pa = _dot(h, wa_ref[:, :n_tok])
    g = (jnp.minimum(z, 0.0) - jnp.log(1.0 + jnp.exp(-jnp.abs(z)))) * (1.0 / GLA_GATE_TEMP)
    gf_ref[0] = g[:, :GLA_KW]
    gb_ref[0] = g[:, GLA_KW:]
    o = 0
    for ref in (q_ref, k_ref, v_ref, r_ref):
        w = ref.shape[-1]
        ref[0] = pa[:, o:o + w].astype(ref.dtype)
        o += w
    n_slabs = hs_ref.shape[0]
    for c in range(n_slabs):
        hs_ref[c] = hf[:, c * LANES:(c + 1) * LANES]
    hp = jnp.concatenate(
        [jnp.concatenate([hs_ref[c, pl.ds(r, tmr, stride=nph), :] for r in range(nph)], axis=0)
         for c in range(n_slabs)], axis=1).astype(bf16)
    x2 = _dot(hp, wa_ref[:, n_tok:]).astype(x2_ref.dtype)
    vt = _dot_nt(wt_ref[...], hp).astype(vt_ref.dtype)
    for r in range(nph):
        for ci in range(x2_ref.shape[2]):
            x2_ref[0, r, ci] = x2[r * tmr:(r + 1) * tmr, ci * HY_CBLK:(ci + 1) * HY_CBLK]
        vt_ref[0, r] = vt[:, r * tmr:(r + 1) * tmr]


def _in_proj(x, nw, wa, wlr, wt, wg, bg, *, tm):
    bsz, seq, d = x.shape
    nph = HY_PHASES
    hw = wt.shape[0] // 2
    sr, tmr = seq // nph, tm // nph
    row = lambda w: pl.BlockSpec((1, tm, w), lambda b, t: (b, t, 0))
    tok = lambda w, dt: jax.ShapeDtypeStruct((bsz, seq, w), dt)
    return pl.pallas_call(
        _in_proj_kernel,
        grid=(bsz, seq // tm),
        in_specs=[row(d)] + [_const_spec(w.shape) for w in (nw, wa, wlr, wt, wg, bg)],
        out_specs=[row(GLA_KW), row(GLA_KW), row(GLA_WIDTH), row(GLA_WIDTH), row(GLA_KW), row(GLA_KW),
                   pl.BlockSpec((1, nph, hw // HY_CBLK, tmr, HY_CBLK), lambda b, t: (b, 0, 0, t, 0)),
                   pl.BlockSpec((1, nph, 2 * hw, tmr), lambda b, t: (b, 0, 0, t))],
        out_shape=[tok(GLA_KW, bf16), tok(GLA_KW, bf16), tok(GLA_WIDTH, bf16), tok(GLA_WIDTH, bf16),
                   tok(GLA_KW, f32), tok(GLA_KW, f32),
                   jax.ShapeDtypeStruct((bsz, nph, hw // HY_CBLK, sr, HY_CBLK), bf16),
                   jax.ShapeDtypeStruct((bsz, nph, 2 * hw, sr), bf16)],
        scratch_shapes=[pltpu.VMEM((d // LANES, tm, LANES), f32)],
        compiler_params=_cparams("parallel", "parallel"),
        name="in_proj",
    )(x, nw, wa, wlr, wt, wg, bg)


def _block_diag(n_blocks, nr, nc):
    rr = np.arange(n_blocks * nr)[:, None] // nr
    cc = np.arange(n_blocks * nc)[None, :] // nc
    return (rr == cc).astype(np.float32)


def _gla_constants(rev):
    c, s, h = GLA_CHUNK, GLA_SUB, GLA_HEADS
    i = np.arange(c)[:, None]
    j = np.arange(c)[None, :]
    tri = (j >= i) if rev else (j <= i)
    lane_j = np.arange(h * c)[None, :] % c
    query_rows = np.zeros((c, len(GLA_LEVELS)), np.float32)
    same = np.zeros((len(GLA_LEVELS), c, h * c), np.float32)
    for l, size in enumerate(GLA_LEVELS):
        upper = (i % size) >= size // 2
        query_rows[:, l:l + 1] = ~upper if rev else upper
        same[l] = (i // size) == (lane_j // size)
    lag = np.zeros((s, c, h * c), np.float32)
    for d in range(s):
        if rev:
            lag[d] = (lane_j == i + d) & (i % s + d <= s - 1)
        else:
            lag[d] = (lane_j == i - d) & (i % s >= d)
    return tri.astype(np.float32), query_rows, same[1:], lag


def _interleave(*stages):
    live = list(stages)
    while live:
        for gen in list(live):
            try:
                next(gen)
            except StopIteration:
                live.remove(gen)


def _gla_chunk(t0, rev, q_ref, k_ref, v_ref, g_ref, st_ref, o_ref, p_ref, sin_ref,
               tri_ref, rows_ref, same_ref, lag_ref, bd2_ref, bd4_ref, ones_bd_ref):
    c, s, nh = GLA_CHUNK, GLA_SUB, GLA_HEADS
    kw = GLA_KW
    q = q_ref[0, pl.ds(t0, c), :].astype(f32)
    k = k_ref[0, pl.ds(t0, c), :].astype(f32)
    v = v_ref[0, pl.ds(t0, c), :]
    g = g_ref[0, pl.ds(t0, c), :]
    g_hi, g_lo = _split_bf16(g)
    tri = tri_ref[...]
    b = _dot(tri, g_hi) + _dot(tri, g_lo)
    yield
    b_end = b[0:1] if rev else b[c - 1:c]

    def stack_heads(x):
        return jnp.concatenate([x] * nh, axis=0)

    st = st_ref[...]
    sin_ref[...] = st.astype(bf16)
    bd4 = bd4_ref[...]
    vst = jnp.concatenate([v[:, h * GLA_DV:(h + 1) * GLA_DV] for h in range(nh)], axis=0)
    kt_bd = stack_heads((k * jnp.exp(b_end - b)).astype(bf16)) * bd4
    st_ref[...] = st * jnp.exp(b_end) + _dot_tn(vst, kt_bd)
    yield

    a = None
    for l, size in enumerate(GLA_LEVELS):
        half = size // 2
        ref_rows = [blk * size + (half if rev else half - 1) for blk in range(c // size)]
        ref = jnp.concatenate([jnp.broadcast_to(b[r:r + 1, :], (size, kw)) for r in ref_rows], axis=0)
        is_q = rows_ref[:, l:l + 1] > 0.5
        e = jnp.exp(jnp.minimum(jnp.where(is_q, b - ref, ref - b), 0.0))
        zero = jnp.zeros_like(q)
        ql = jnp.where(is_q, q * e, zero).astype(bf16)
        kl = jnp.where(is_q, zero, k * e).astype(bf16)
        parts = []
        for p in range(kw // LANES):
            sl = slice(p * LANES, (p + 1) * LANES)
            kbd = jnp.concatenate([kl[:, sl]] * 2, axis=0) * bd2_ref[...]
            parts.append(_dot_nt(ql[:, sl], kbd))
        al = jnp.concatenate(parts, axis=1)
        if l > 0:
            al = al * same_ref[l - 1]
        a = al if a is None else a + al
        yield

    gam = jnp.exp(g)
    x = k
    for d in range(s):
        if d > 0:
            x3 = x.reshape(c // s, s, kw)
            x = pltpu.roll(x3, (s - 1) if rev else 1, axis=1).reshape(c, kw) * gam
        p_ref[d * c:(d + 1) * c, :] = (q * x).astype(bf16)
    red = _dot(p_ref[...], ones_bd_ref[...])
    yield
    for d in range(s):
        a = a + red[d * c:(d + 1) * c, :] * lag_ref[d]

    a_st = stack_heads(a.astype(bf16)) * bd4
    qs_st = stack_heads((q * jnp.exp(b)).astype(bf16)) * bd4
    o_st = _dot(a_st, vst) + _dot_nt(qs_st, sin_ref[...])
    yield
    for h in range(nh):
        o_ref[pl.ds(t0, c), h * GLA_DV:(h + 1) * GLA_DV] = o_st[h * c:(h + 1) * c, :]


def _gla_kernel(q_ref, k_ref, v_ref, gf_ref, gb_ref,
                trif_ref, rowsf_ref, samef_ref, lagf_ref, trib_ref, rowsb_ref, sameb_ref, lagb_ref,
                bd2_ref, bd4_ref, ones_bd_ref,
                out_ref,
                of_ref, ob_ref, stf_ref, stb_ref, p_ref, sin_ref, *, seq, rows):
    c = GLA_CHUNK
    stf_ref[...] = jnp.zeros_like(stf_ref)
    stb_ref[...] = jnp.zeros_like(stb_ref)

    consts = (bd2_ref, bd4_ref, ones_bd_ref)
    n_chunks = seq // c

    def scan(i, carry):
        chains = []
        for u in range(GLA_UNROLL):
            n = i * GLA_UNROLL + u
            tf = pl.multiple_of(n * c, c)
            tb = pl.multiple_of((n_chunks - 1 - n) * c, c)
            chains.append(_gla_chunk(tf, False, q_ref, k_ref, v_ref, gf_ref, stf_ref, of_ref,
                                     p_ref.at[2 * u], sin_ref.at[2 * u],
                                     trif_ref, rowsf_ref, samef_ref, lagf_ref, *consts))
            chains.append(_gla_chunk(tb, True, q_ref, k_ref, v_ref, gb_ref, stb_ref, ob_ref,
                                     p_ref.at[2 * u + 1], sin_ref.at[2 * u + 1],
                                     trib_ref, rowsb_ref, sameb_ref, lagb_ref, *consts))
        _interleave(*chains)
        return carry

    lax.fori_loop(0, n_chunks // GLA_UNROLL, scan, 0)

    def finish(i, carry):
        r0 = pl.multiple_of(i * rows, rows)
        o = of_ref[pl.ds(r0, rows), :] + ob_ref[pl.ds(r0, rows), :]
        out_ref[0, pl.ds(r0, rows), :] = o.astype(out_ref.dtype)
        return carry

    lax.fori_loop(0, seq // rows, finish, 0)


def _gla(q, k, v, gf, gb):
    bsz, seq, _ = q.shape
    c, s, h = GLA_CHUNK, GLA_SUB, GLA_HEADS
    consts = []
    for rev in (False, True):
        tri, rows, same, lag = _gla_constants(rev)
        consts += [jnp.asarray(tri, bf16), jnp.asarray(rows), jnp.asarray(same), jnp.asarray(lag)]
    consts += [jnp.asarray(_block_diag(2, c, GLA_DK), bf16),
               jnp.asarray(_block_diag(h, c, GLA_DK), bf16),
               jnp.asarray(_block_diag(h, GLA_DK, c), bf16)]
    per_b = lambda w: pl.BlockSpec((1, seq, w), lambda b: (b, 0, 0))
    return pl.pallas_call(
        functools.partial(_gla_kernel, seq=seq, rows=256),
        grid=(bsz,),
        in_specs=[per_b(GLA_KW), per_b(GLA_KW), per_b(GLA_WIDTH), per_b(GLA_KW), per_b(GLA_KW)]
        + [_const_spec(x.shape) for x in consts],
        out_specs=per_b(GLA_WIDTH),
        out_shape=jax.ShapeDtypeStruct((bsz, seq, GLA_WIDTH), bf16),
        scratch_shapes=[pltpu.VMEM((seq, GLA_WIDTH), f32), pltpu.VMEM((seq, GLA_WIDTH), f32),
                        pltpu.VMEM((GLA_DV, GLA_KW), f32), pltpu.VMEM((GLA_DV, GLA_KW), f32),
                        pltpu.VMEM((2 * GLA_UNROLL, s * c, GLA_KW), bf16),
                        pltpu.VMEM((2 * GLA_UNROLL, GLA_DV, GLA_KW), bf16)],
        compiler_params=_cparams("parallel"),
        name="gla",
    )(q, k, v, gf, gb, *consts)


def _short_conv_phases(u, w, axis):
    nph = len(u)
    n = u[0].shape[axis]
    m = lax.broadcasted_iota(jnp.int32, u[0].shape, axis)
    before_first = jnp.where(m == 0, 0.0, pltpu.roll(u[nph - 1], 1, axis=axis))
    after_last = jnp.where(m == n - 1, 0.0, pltpu.roll(u[0], n - 1, axis=axis))
    out = []
    for r in range(nph):
        prev = u[r - 1] if r > 0 else before_first
        nxt = u[r + 1] if r < nph - 1 else after_last
        out.append(prev * w[0] + u[r] * w[1] + nxt * w[2] + w[3])
    return out


def _hyena_kernel(v_ref, x1_ref, x2_ref, cwt_v_ref, cwt_x1_ref, cw_x2_ref, a_ref, b_ref, corr_ref,
                  wre_ref, wim_ref, out_ref, src_ref, acc1_ref, acc2_ref, u_ref, y_ref, tok_ref):
    order = pl.program_id(2)
    ft = pl.program_id(3)
    n_ft = pl.num_programs(3)
    nb, nph, cb, ls = v_ref.shape
    rows = nb * cb

    rb = HY_RB

    def conv_to_src(u_ref_in, w_ref, gate_ref):
        def body(i, carry):
            bi = i // (cb // rb)
            crow = pl.multiple_of((i % (cb // rb)) * rb, rb)
            row0 = pl.multiple_of(i * rb, rb)
            w = w_ref[pl.ds(crow, rb), :]
            u = [u_ref_in[bi, r, pl.ds(crow, rb), :].astype(f32) for r in range(nph)]
            c = _short_conv_phases(u, [w[:, j:j + 1] for j in range(4)], axis=1)
            for r in range(nph):
                val = c[r] if gate_ref is None else c[r] * gate_ref[r, pl.ds(row0, rb), :]
                src_ref[r, pl.ds(row0, rb), :] = val.astype(bf16)
            return carry
        lax.fori_loop(0, rows // rb, body, 0, unroll=4)

    @pl.when(jnp.logical_and(order == 0, ft == 0))
    def _():
        conv_to_src(v_ref, cwt_v_ref, None)
        acc1_ref[...] = jnp.zeros_like(acc1_ref)

    @pl.when(jnp.logical_and(order == 1, ft == 0))
    def _():
        conv_to_src(x1_ref, cwt_x1_ref, acc1_ref)
        acc2_ref[...] = jnp.zeros_like(acc2_ref)

    wre = wre_ref[0]
    wim = wim_ref[0]
    ftile = wre.shape[1]
    for r in range(nph):
        u_ref[2 * r] = _dot(src_ref[r], wre).astype(bf16)
        u_ref[2 * r + 1] = _dot(src_ref[r], wim).astype(bf16)
    first_tile = (ft == 0).astype(f32).astype(bf16)

    rb = HY_RB

    def combine(i, carry):
        row0 = pl.multiple_of(i * rb, rb)
        crow = pl.multiple_of((i % (cb // rb)) * rb, rb)
        for lane0 in range(0, ftile, LANES):
            cols = slice(lane0, lane0 + LANES)
            u_re = [u_ref[2 * r, pl.ds(row0, rb), cols] for r in range(nph)]
            u_im = [u_ref[2 * r + 1, pl.ds(row0, rb), cols] for r in range(nph)]
            for rp in range(nph):
                y_re = y_im = None
                for r in range(nph):
                    d = rp - r + nph - 1
                    a = a_ref[0, 0, 0, d, pl.ds(crow, rb), cols]
                    b = b_ref[0, 0, 0, d, pl.ds(crow, rb), cols]
                    t_re = u_re[r] * a - u_im[r] * b
                    t_im = u_re[r] * b + u_im[r] * a
                    if lane0 == 0:
                        t_im = t_im + u_im[r] * (corr_ref[0, 0, d, pl.ds(crow, rb), :] * first_tile)
                    y_re = t_re if y_re is None else y_re + t_re
                    y_im = t_im if y_im is None else y_im + t_im
                y_ref[2 * rp, pl.ds(row0, rb), cols] = y_re
                y_ref[2 * rp + 1, pl.ds(row0, rb), cols] = y_im
        return carry

    lax.fori_loop(0, rows // rb, combine, 0)

    @pl.when(order == 0)
    def _():
        for rp in range(nph):
            acc1_ref[rp] += _dot_nt(y_ref[2 * rp], wre) + _dot_nt(y_ref[2 * rp + 1], wim)

    @pl.when(order == 1)
    def _():
        for rp in range(nph):
            acc2_ref[rp] += _dot_nt(wre, y_ref[2 * rp]) + _dot_nt(wim, y_ref[2 * rp + 1])

    @pl.when(jnp.logical_and(order == 1, ft == n_ft - 1))
    def _():
        w = cw_x2_ref[...]
        taps = [w[i:i + 1] for i in range(4)]
        for i in range(nb):
            x2 = _short_conv_phases([x2_ref[i, r, 0].astype(f32) for r in range(nph)], taps, axis=0)
            for r in range(nph):
                y = x2[r] * acc2_ref[r][:, i * cb:(i + 1) * cb]
                for c in range(cb // LANES):
                    tok_ref[c, pl.ds(r, ls, stride=nph), :] = y[:, c * LANES:(c + 1) * LANES]
            for c in range(cb // LANES):
                out_ref[i, 0, :, c * LANES:(c + 1) * LANES] = tok_ref[c].astype(out_ref.dtype)


def _hyena(vx1t, x2, cwt, cw_x2, fa, fb, fcorr, ws):
    bsz, nph, two_w, ls = vx1t.shape
    width = two_w // 2
    cb, nb, ftile = HY_CBLK, HY_BBLK, HY_FTILE
    n_cb = width // cb
    n_ft = ls // ftile
    nd = fa.shape[3]
    rows = nb * cb
    filt = lambda w: pl.BlockSpec((1, 1, 1, nd, cb, w), lambda c, b, o, f: (o, c, f, 0, 0, 0))
    return pl.pallas_call(
        _hyena_kernel,
        grid=(n_cb, bsz // nb, HY_ORDER, n_ft),
        in_specs=[pl.BlockSpec((nb, nph, cb, ls), lambda c, b, o, f: (b, 0, c, 0)),
                  pl.BlockSpec((nb, nph, cb, ls), lambda c, b, o, f: (b, 0, n_cb + c, 0)),
                  pl.BlockSpec((nb, nph, 1, ls, cb), lambda c, b, o, f: (b, 0, c, 0, 0)),
                  pl.BlockSpec((cb, 4), lambda c, b, o, f: (c, 0)),
                  pl.BlockSpec((cb, 4), lambda c, b, o, f: (n_cb + c, 0)),
                  pl.BlockSpec((4, cb), lambda c, b, o, f: (0, c)),
                  filt(ftile), filt(ftile),
                  pl.BlockSpec((1, 1, nd, cb, LANES), lambda c, b, o, f: (o, c, 0, 0, 0)),
                  pl.BlockSpec((1, ls, ftile), lambda c, b, o, f: (f, 0, 0)),
                  pl.BlockSpec((1, ls, ftile), lambda c, b, o, f: (n_ft + f, 0, 0))],
        out_specs=pl.BlockSpec((nb, 1, nph * ls, cb), lambda c, b, o, f: (b, c, 0, 0)),
        out_shape=jax.ShapeDtypeStruct((bsz, n_cb, nph * ls, cb), bf16),
        scratch_shapes=[pltpu.VMEM((nph, rows, ls), bf16), pltpu.VMEM((nph, rows, ls), f32),
                        pltpu.VMEM((nph, ls, rows), f32),
                        pltpu.VMEM((2 * nph, rows, ftile), bf16), pltpu.VMEM((2 * nph, rows, ftile), bf16),
                        pltpu.VMEM((cb // LANES, nph * ls, LANES), f32)],
        compiler_params=_cparams("parallel", "parallel", "arbitrary", "arbitrary"),
        name="hyena",
    )(vx1t, vx1t, x2, cwt, cwt, cw_x2, fa, fb, fcorr, ws, ws)


def _out_ffn_kernel(x_ref, og_ref, r_ref, hy_ref, gnw_ref, hnw_ref, wo_g_ref, wo_h_ref, fnw_ref, wg_ref, wu_ref,
                    wd_ref, *rest, final_norm):
    if final_norm:
        lnw_ref, out_ref = rest
    else:
        (out_ref,) = rest
    o = og_ref[...].astype(f32)
    gate = _silu(r_ref[...].astype(f32))
    gnw = gnw_ref[...]
    og = jnp.concatenate([_rms(o[:, h * GLA_DV:(h + 1) * GLA_DV]) * gnw for h in range(GLA_HEADS)], axis=1)
    og = (og * gate).astype(bf16)
    hy = jnp.concatenate([hy_ref[0, ci] for ci in range(hy_ref.shape[1])], axis=1)
    hy = (_rms(hy.astype(f32)) * hnw_ref[...]).astype(bf16)
    x = x_ref[...] + _dot(og, wo_g_ref[...]) + _dot(hy, wo_h_ref[...])
    h = (_rms(x) * fnw_ref[...]).astype(bf16)
    ff = (_silu(_dot(h, wg_ref[...])) * _dot(h, wu_ref[...])).astype(bf16)
    x = x + _dot(ff, wd_ref[...])
    if final_norm:
        x = _rms(x) * lnw_ref[...]
    out_ref[...] = x


def _out_ffn(x, og, r, hy, gnw, hnw, wo_g, wo_h, fnw, wg, wu, wd, lnw, *, tm):
    n, d = x.shape
    _, n_cb, seq, cb = hy.shape
    tiles_per_b = seq // tm
    row = lambda w: pl.BlockSpec((tm, w), lambda i: (i, 0))
    weights = [gnw, hnw, wo_g, wo_h, fnw, wg, wu, wd] + ([] if lnw is None else [lnw])
    return pl.pallas_call(
        functools.partial(_out_ffn_kernel, final_norm=lnw is not None),
        grid=(n // tm,),
        in_specs=[row(d), row(og.shape[1]), row(r.shape[1]),
                  pl.BlockSpec((1, n_cb, tm, cb), lambda i: (i // tiles_per_b, 0, i % tiles_per_b, 0))]
        + [_const_spec(w.shape) for w in weights],
        out_specs=row(d),
        out_shape=jax.ShapeDtypeStruct((n, d), f32),
        compiler_params=_cparams("parallel"),
        name="out_ffn",
    )(x, og, r, hy, *weights)


def _prepare_layer(norm_mix, w_in, wg_f, bg_f, wg_b, bg_b, gla_norm, conv_w, conv_b, hy_norm, w_out,
                   norm_ffn, w_gate, w_up, w_down):
    d = w_in.shape[0]
    hw = hy_norm.shape[0]
    o_lr = 2 * GLA_KW + 2 * GLA_WIDTH
    o_hy = o_lr + 2 * GLA_GATE_RANK
    wq = w_in[:, :GLA_KW] * (GLA_DK ** -0.5)
    wa = jnp.concatenate([wq, w_in[:, GLA_KW:o_lr], w_in[:, o_hy + 2 * hw:]], axis=1).astype(bf16)
    wlr = jnp.zeros((d, LANES), f32).at[:, :2 * GLA_GATE_RANK].set(w_in[:, o_lr:o_hy]).astype(bf16)
    wt = w_in[:, o_hy:o_hy + 2 * hw].T.astype(bf16)
    wg = jnp.zeros((LANES, 2 * GLA_KW), f32)
    wg = wg.at[:GLA_GATE_RANK, :GLA_KW].set(wg_f).at[GLA_GATE_RANK:2 * GLA_GATE_RANK, GLA_KW:].set(wg_b)
    wg = wg.astype(bf16)
    bg = jnp.concatenate([bg_f, bg_b])[None, :]
    cw = jnp.concatenate([conv_w, conv_b[None, :]], axis=0)
    return dict(
        nw=norm_mix[None, :], wa=wa, wlr=wlr, wt=wt, wgate=wg, bg=bg, gnw=gla_norm[None, :],
        cwt=cw[:, :2 * hw].T, cw_x2=cw[:, 2 * hw:], hnw=hy_norm[None, :],
        wo_g=w_out[:GLA_WIDTH].astype(bf16), wo_h=w_out[GLA_WIDTH:].astype(bf16),
        fnw=norm_ffn[None, :], wg=w_gate.astype(bf16), wu=w_up.astype(bf16), wd=w_down.astype(bf16))


def _layer(x, p, filt, wf, lnw):
    q, k, v, r, gf, gb, x2, vx1t = _in_proj(x, p["nw"], p["wa"], p["wlr"], p["wt"], p["wgate"], p["bg"],
                                            tm=512)
    og = _gla(q, k, v, gf, gb)
    hy = _hyena(vx1t, x2, p["cwt"], p["cw_x2"], *filt, wf)
    bsz, seq, d = x.shape
    n = bsz * seq
    y = _out_ffn(x.reshape(n, d), og.reshape(n, -1), r.reshape(n, -1), hy, p["gnw"], p["hnw"],
                 p["wo_g"], p["wo_h"], p["fnw"], p["wg"], p["wu"], p["wd"], lnw, tm=512)
    return y.reshape(bsz, seq, d)


def kernel(x_prompt, x_sample, norm_mix, w_in, gla_wg_f, gla_bg_f, gla_wg_b, gla_bg_b, gla_norm, hy_conv_w, hy_conv_b, hy_w1, hy_b1, hy_freq1, hy_w2, hy_b2, hy_freq2, hy_w3, hy_skip, hy_norm, w_out, norm_ffn, w_gate, w_up, w_down, norm_final):
    depth = w_in.shape[0]
    seq = x_prompt.shape[1]
    assert x_sample.shape[1] == seq
    width = hy_norm.shape[1]
    wf = _dft_matrix(seq // HY_PHASES)
    feats, lag = _positional_features(seq)
    layers, filters = [], []
    for l in range(depth):
        layers.append(_prepare_layer(norm_mix[l], w_in[l], gla_wg_f[l], gla_bg_f[l], gla_wg_b[l], gla_bg_b[l],
                                     gla_norm[l], hy_conv_w[l], hy_conv_b[l], hy_norm[l], w_out[l],
                                     norm_ffn[l], w_gate[l], w_up[l], w_down[l]))
        filters.append(_hyena_filters(feats, lag, hy_w1[l], hy_b1[l], hy_freq1[l], hy_w2[l], hy_b2[l], hy_freq2[l],
                                      hy_w3[l], hy_skip[l], wf, seq=seq, width=width))
    outs = []
    for x in (x_prompt, x_sample):
        for l in range(depth):
            x = _layer(x, layers[l], filters[l], wf, norm_final[None, :] if l == depth - 1 else None)
        outs.append(x)
    return tuple(outs)
```

```python
import functools
import math

import numpy as np
import jax
import jax.numpy as jnp
from jax import lax
from jax.experimental import pallas as pl
from jax.experimental.pallas import tpu as pltpu

f32 = jnp.float32
bf16 = jnp.bfloat16

LANES = 128
SUBLANES = 8
VMEM_LIMIT_BYTES = 56 * 1024 * 1024

NORM_EPS = 1e-6

GLA_HEADS = 4
GLA_DK = 64
GLA_DV = 128
GLA_KW = GLA_HEADS * GLA_DK
GLA_WIDTH = GLA_HEADS * GLA_DV
GLA_GATE_RANK = 16
GLA_GATE_TEMP = 16.0
GLA_CHUNK = 64
GLA_LEVELS = (64, 32, 16)
GLA_SUB = SUBLANES
GLA_UNROLL = 4

HY_ORDER = 2
HY_DIRS = 2
HY_BANDS = 8
HY_EMB = 1 + 2 * HY_BANDS
HY_FFN = 64
HY_FAST_DECAY = 0.3
HY_SLOW_DECAY = 1.5
HY_DECAY_TARGET = 1e-2
HY_PHASES = 4
HY_FTILE = 512
HY_CBLK = 128
HY_BBLK = 4
HY_RB = 32


def _cparams(*sem):
    return pltpu.CompilerParams(dimension_semantics=sem, vmem_limit_bytes=VMEM_LIMIT_BYTES)


def _const_spec(shape):
    nd = len(shape)
    return pl.BlockSpec(shape, lambda *_: (0,) * nd, pipeline_mode=pl.Buffered(1))


def _split_bf16(a):
    hi = a.astype(bf16)
    lo = (a - hi.astype(f32)).astype(bf16)
    return hi, lo


def _dot(a, b):
    return jnp.dot(a, b, preferred_element_type=f32)


def _dot_nt(a, b):
    return lax.dot_general(a, b, (((1,), (1,)), ((), ())), preferred_element_type=f32)


def _dot_tn(a, b):
    return lax.dot_general(a, b, (((0,), (0,)), ((), ())), preferred_element_type=f32)


def _dot3(a_hi, a_lo, b):
    b_hi, b_lo = _split_bf16(b)
    return _dot(a_hi, b_hi) + (_dot(a_lo, b_hi) + _dot(a_hi, b_lo))


def _dot2(a, w):
    a_hi, a_lo = _split_bf16(a)
    return _dot(a_hi, w) + _dot(a_lo, w)


def _rms(x):
    return x * lax.rsqrt(jnp.mean(x * x, axis=-1, keepdims=True) + NORM_EPS)


def _silu(x):
    return x / (1.0 + jnp.exp(-x))


def _dft_kernel(o_ref, *, n_fft, rows):
    half = n_fft // 2
    i = pl.program_id(0)
    n = lax.broadcasted_iota(jnp.int32, (rows, half), 0) + i * rows
    f = lax.broadcasted_iota(jnp.int32, (rows, half), 1)
    ang = ((n * f) & (n_fft - 1)).astype(f32) * (2.0 * math.pi / n_fft)
    re = jnp.cos(ang).astype(o_ref.dtype)
    nyq = jnp.where((n & 1) == 0, 1.0, -1.0)
    im = jnp.where(f == 0, nyq, -jnp.sin(ang)).astype(o_ref.dtype)
    ftile = o_ref.shape[2]
    n_ft = half // ftile
    for t in range(n_ft):
        o_ref[t] = re[:, t * ftile:(t + 1) * ftile]
        o_ref[n_ft + t] = im[:, t * ftile:(t + 1) * ftile]


def _dft_matrix(seq):
    rows = 256
    n_tiles = 2 * seq // HY_FTILE
    return pl.pallas_call(
        functools.partial(_dft_kernel, n_fft=2 * seq, rows=rows),
        grid=(seq // rows,),
        out_specs=pl.BlockSpec((n_tiles, rows, HY_FTILE), lambda i: (0, i, 0)),
        out_shape=jax.ShapeDtypeStruct((n_tiles, seq, HY_FTILE), bf16),
        compiler_params=_cparams("parallel"),
        name="dft_matrix",
    )()


def _filter_kernel(feats_ref, lag_ref, w1h_ref, w1l_ref, b1_ref, fr1_ref, w2h_ref, w2l_ref, b2_ref, fr2_ref,
                   w3fh_ref, w3fl_ref, w3bh_ref, w3bl_ref, delta_ref, skip_ref, ws_ref,
                   a_ref, b_ref, corr_ref, *, seq):
    nph = HY_PHASES
    ls = seq // nph
    ns = 2 * ls
    feats = feats_ref[...]
    hid = jnp.sin(fr1_ref[...] * (_dot3(w1h_ref[...], w1l_ref[...], feats) + b1_ref[...]))
    hid = jnp.sin(fr2_ref[...] * (_dot3(w2h_ref[...], w2l_ref[...], hid) + b2_ref[...]))
    fwd = _dot3(w3fh_ref[0], w3fl_ref[0], hid)
    bwd = _dot3(w3bh_ref[0], w3bl_ref[0], hid)
    lag = lag_ref[...]
    window = jnp.exp(-(lag * (1.0 / (seq - 1))) * delta_ref[...])
    fwd = fwd * window
    bwd = jnp.where(lag == 0.0, 0.0, bwd * window)
    mu = lax.broadcasted_iota(jnp.int32, (fwd.shape[0], ls), 1)
    first = mu == 0

    def phase(x, p):
        return x[:, p * ls:(p + 1) * ls]

    def delayed(x):
        return jnp.where(first, 0.0, pltpu.roll(x, 1, axis=1))

    skip = skip_ref[0]
    lane = lax.broadcasted_iota(jnp.int32, (fwd.shape[0], LANES), 1)
    n_ft = ws_ref.shape[0] // 2
    ftile = ws_ref.shape[2]
    ws_re = jnp.concatenate([ws_ref[t] for t in range(n_ft)], axis=1)
    ws_im = jnp.concatenate([ws_ref[n_ft + t] for t in range(n_ft)], axis=1)
    for d in range(-(nph - 1), nph):
        if d >= 0:
            pos = phase(fwd, d)
            neg = phase(bwd, 0) if d == 0 else delayed(phase(bwd, nph - d))
        else:
            pos = jnp.where(first, phase(bwd, -d), delayed(phase(fwd, nph + d)))
            neg = jnp.where(first, 0.0, phase(bwd, -d))
        sym = pos + neg
        g_re = _dot2(sym, ws_re)
        g_im = _dot2(pos - neg, ws_im)
        g_ny = _dot2(sym, ws_im[:, :LANES])[:, 0:1]
        if d == 0:
            g_re = g_re + skip
            g_ny = g_ny + skip
        a = g_re * jnp.where(first, 1.0 / ns, 2.0 / ns)
        b = jnp.where(first, 0.0, g_im * (2.0 / ns))
        for t in range(n_ft):
            a_ref[0, 0, t, d + nph - 1] = a[:, t * ftile:(t + 1) * ftile].astype(a_ref.dtype)
            b_ref[0, 0, t, d + nph - 1] = b[:, t * ftile:(t + 1) * ftile].astype(b_ref.dtype)
        corr = jnp.where(lane == 0, g_ny * (1.0 / ns) - a[:, 0:1], 0.0)
        corr_ref[0, 0, d + nph - 1] = corr.astype(corr_ref.dtype)


def _hyena_filters(feats, lag, w1, b1, fr1, w2, b2, fr2, w3, skip, ws, *, seq, width):
    pad = LANES
    nd = 2 * HY_PHASES - 1
    ls = seq // HY_PHASES

    def pad2(m, r, c):
        return jnp.zeros((r, c), f32).at[:m.shape[0], :m.shape[1]].set(m)

    w1h, w1l = _split_bf16(pad2(w1.T, pad, pad))
    w2h, w2l = _split_bf16(pad2(w2.T, pad, pad))
    w3t = pad2(w3.T, w3.shape[1], pad).reshape(HY_ORDER, HY_DIRS, width, pad)
    w3fh, w3fl = _split_bf16(w3t[:, 0])
    w3bh, w3bl = _split_bf16(w3t[:, 1])
    col = lambda v: pad2(v[:, None], pad, 1)
    max_decay = math.log(HY_DECAY_TARGET) / HY_FAST_DECAY
    min_decay = math.log(HY_DECAY_TARGET) / HY_SLOW_DECAY
    delta = jnp.abs(jnp.linspace(min_decay, max_decay, width, dtype=f32))[:, None]
    cb = HY_CBLK
    full = lambda shape: pl.BlockSpec(shape, lambda o, c: (0,) * len(shape))
    w3spec = pl.BlockSpec((1, cb, pad), lambda o, c: (o, c, 0))
    n_cb, n_ft = width // cb, ls // HY_FTILE
    out_spec = pl.BlockSpec((1, 1, n_ft, nd, cb, HY_FTILE), lambda o, c: (o, c, 0, 0, 0, 0))
    return pl.pallas_call(
        functools.partial(_filter_kernel, seq=seq),
        grid=(HY_ORDER, width // cb),
        in_specs=[full((pad, seq)), full((1, seq)),
                  full((pad, pad)), full((pad, pad)), full((pad, 1)), full((pad, 1)),
                  full((pad, pad)), full((pad, pad)), full((pad, 1)), full((pad, 1)),
                  w3spec, w3spec, w3spec, w3spec,
                  pl.BlockSpec((cb, 1), lambda o, c: (c, 0)),
                  pl.BlockSpec((1, cb, 1), lambda o, c: (o, c, 0)),
                  _const_spec(ws.shape)],
        out_specs=[out_spec, out_spec, pl.BlockSpec((1, 1, nd, cb, LANES), lambda o, c: (o, c, 0, 0, 0))],
        out_shape=[jax.ShapeDtypeStruct((HY_ORDER, n_cb, n_ft, nd, cb, HY_FTILE), bf16),
                   jax.ShapeDtypeStruct((HY_ORDER, n_cb, n_ft, nd, cb, HY_FTILE), bf16),
                   jax.ShapeDtypeStruct((HY_ORDER, n_cb, nd, cb, LANES), bf16)],
        compiler_params=_cparams("arbitrary", "arbitrary"),
        name="hyena_filters",
    )(feats, lag, w1h, w1l, col(b1), col(fr1), w2h, w2l, col(b2), col(fr2),
      w3fh, w3fl, w3bh, w3bl, delta, skip[:, :, None], ws)


def _positional_features(seq):
    ls = seq // HY_PHASES
    col = np.arange(seq)
    lag = jnp.asarray(HY_PHASES * (col % ls) + col // ls, f32)[:, None]
    t = lag * (1.0 / (seq - 1))
    pos_w = (2.0 * math.pi / seq) * lag
    bands = jnp.linspace(1e-4, HY_BANDS - 1.0, HY_BANDS, dtype=f32)
    feats = jnp.concatenate([t, jnp.cos(bands * pos_w), -jnp.sin(bands * pos_w)], axis=-1)
    feats_t = jnp.zeros((LANES, seq), f32).at[:HY_EMB].set(feats.T)
    return feats_t, lag.T


def _in_proj_kernel(x_ref, nw_ref, wa_ref, wlr_ref, wt_ref, wg_ref, bg_ref,
                    q_ref, k_ref, v_ref, r_ref, gf_ref, gb_ref, x2_ref, vt_ref, hs_ref):
    nph = HY_PHASES
    tm = x_ref.shape[1]
    tmr = tm // nph
    hf = _rms(x_ref[0]) * nw_ref[...]
    h = hf.astype(bf16)
    z = _dot(_dot(h, wlr_ref[...]).astype(bf16), wg_ref[...]) + bg_ref[...]
    n_tok = wa_ref.shape[1] - x2_ref.shape[2] * x2_ref.shape[4]
    pa = _dot(h, wa_ref[:, :n_tok])
    g = (jnp.minimum(z, 0.0) - jnp.log(1.0 + jnp.exp(-jnp.abs(z)))) * (1.0 / GLA_GATE_TEMP)
    gf_ref[0] = g[:, :GLA_KW]
    gb_ref[0] = g[:, GLA_KW:]
    o = 0
    for ref in (q_ref, k_ref, v_ref, r_ref):
        w = ref.shape[-1]
        ref[0] = pa[:, o:o + w].astype(ref.dtype)
        o += w
    n_slabs = hs_ref.shape[0]
    for c in range(n_slabs):
        hs_ref[c] = hf[:, c * LANES:(c + 1) * LANES]
    hp = jnp.concatenate(
        [jnp.concatenate([hs_ref[c, pl.ds(r, tmr, stride=nph), :] for r in range(nph)], axis=0)
         for c in range(n_slabs)], axis=1).astype(bf16)
    x2 = _dot(hp, wa_ref[:, n_tok:]).astype(x2_ref.dtype)
    vt = _dot_nt(wt_ref[...], hp).astype(vt_ref.dtype)
    for r in range(nph):
        for ci in range(x2_ref.shape[2]):
            x2_ref[0, r, ci] = x2[r * tmr:(r + 1) * tmr, ci * HY_CBLK:(ci + 1) * HY_CBLK]
        vt_ref[0, r] = vt[:, r * tmr:(r + 1) * tmr]


def _in_proj(x, nw, wa, wlr, wt, wg, bg, *, tm):
    bsz, seq, d = x.shape
    nph = HY_PHASES
    hw = wt.shape[0] // 2
    sr, tmr = seq // nph, tm // nph
    row = lambda w: pl.BlockSpec((1, tm, w), lambda b, t: (b, t, 0))
    tok = lambda w, dt: jax.ShapeDtypeStruct((bsz, seq, w), dt)
    return pl.pallas_call(
        _in_proj_kernel,
        grid=(bsz, seq // tm),
        in_specs=[row(d)] + [_const_spec(w.shape) for w in (nw, wa, wlr, wt, wg, bg)],
        out_specs=[row(GLA_KW), row(GLA_KW), row(GLA_WIDTH), row(GLA_WIDTH), row(GLA_KW), row(GLA_KW),
                   pl.BlockSpec((1, nph, hw // HY_CBLK, tmr, HY_CBLK), lambda b, t: (b, 0, 0, t, 0)),
                   pl.BlockSpec((1, nph, 2 * hw, tmr), lambda b, t: (b, 0, 0, t))],
        out_shape=[tok(GLA_KW, bf16), tok(GLA_KW, bf16), tok(GLA_WIDTH, bf16), tok(GLA_WIDTH, bf16),
                   tok(GLA_KW, f32), tok(GLA_KW, f32),
                   jax.ShapeDtypeStruct((bsz, nph, hw // HY_CBLK, sr, HY_CBLK), bf16),
                   jax.ShapeDtypeStruct((bsz, nph, 2 * hw, sr), bf16)],
        scratch_shapes=[pltpu.VMEM((d // LANES, tm, LANES), f32)],
        compiler_params=_cparams("parallel", "parallel"),
        name="in_proj",
    )(x, nw, wa, wlr, wt, wg, bg)


def _block_diag(n_blocks, nr, nc):
    rr = np.arange(n_blocks * nr)[:, None] // nr
    cc = np.arange(n_blocks * nc)[None, :] // nc
    return (rr == cc).astype(np.float32)


def _gla_constants(rev):
    c, s, h = GLA_CHUNK, GLA_SUB, GLA_HEADS
    i = np.arange(c)[:, None]
    j = np.arange(c)[None, :]
    tri = (j >= i) if rev else (j <= i)
    lane_j = np.arange(h * c)[None, :] % c
    query_rows = np.zeros((c, len(GLA_LEVELS)), np.float32)
    same = np.zeros((len(GLA_LEVELS), c, h * c), np.float32)
    for l, size in enumerate(GLA_LEVELS):
        upper = (i % size) >= size // 2
        query_rows[:, l:l + 1] = ~upper if rev else upper
        same[l] = (i // size) == (lane_j // size)
    lag = np.zeros((s, c, h * c), np.float32)
    for d in range(s):
        if rev:
            lag[d] = (lane_j == i + d) & (i % s + d <= s - 1)
        else:
            lag[d] = (lane_j == i - d) & (i % s >= d)
    return tri.astype(np.float32), query_rows, same[1:], lag


def _interleave(*stages):
    live = list(stages)
    while live:
        for gen in list(live):
            try:
                next(gen)
            except StopIteration:
                live.remove(gen)


def _gla_chunk(t0, rev, q_ref, k_ref, v_ref, g_ref, st_ref, o_ref, p_ref, sin_ref,
               tri_ref, rows_ref, same_ref, lag_ref, bd2_ref, bd4_ref, ones_bd_ref):
    c, s, nh = GLA_CHUNK, GLA_SUB, GLA_HEADS
    kw = GLA_KW
    q = q_ref[0, pl.ds(t0, c), :].astype(f32)
    k = k_ref[0, pl.ds(t0, c), :].astype(f32)
    v = v_ref[0, pl.ds(t0, c), :]
    g = g_ref[0, pl.ds(t0, c), :]
    g_hi, g_lo = _split_bf16(g)
    tri = tri_ref[...]
    b = _dot(tri, g_hi) + _dot(tri, g_lo)
    yield
    b_end = b[0:1] if rev else b[c - 1:c]

    def stack_heads(x):
        return jnp.concatenate([x] * nh, axis=0)

    st = st_ref[...]
    sin_ref[...] = st.astype(bf16)
    bd4 = bd4_ref[...]
    vst = jnp.concatenate([v[:, h * GLA_DV:(h + 1) * GLA_DV] for h in range(nh)], axis=0)
    kt_bd = stack_heads((k * jnp.exp(b_end - b)).astype(bf16)) * bd4
    st_ref[...] = st * jnp.exp(b_end) + _dot_tn(vst, kt_bd)
    yield

    a = None
    for l, size in enumerate(GLA_LEVELS):
        half = size // 2
        ref_rows = [blk * size + (half if rev else half - 1) for blk in range(c // size)]
        ref = jnp.concatenate([jnp.broadcast_to(b[r:r + 1, :], (size, kw)) for r in ref_rows], axis=0)
        is_q = rows_ref[:, l:l + 1] > 0.5
        e = jnp.exp(jnp.minimum(jnp.where(is_q, b - ref, ref - b), 0.0))
        zero = jnp.zeros_like(q)
        ql = jnp.where(is_q, q * e, zero).astype(bf16)
        kl = jnp.where(is_q, zero, k * e).astype(bf16)
        parts = []
        for p in range(kw // LANES):
            sl = slice(p * LANES, (p + 1) * LANES)
            kbd = jnp.concatenate([kl[:, sl]] * 2, axis=0) * bd2_ref[...]
            parts.append(_dot_nt(ql[:, sl], kbd))
        al = jnp.concatenate(parts, axis=1)
        if l > 0:
            al = al * same_ref[l - 1]
        a = al if a is None else a + al
        yield

    gam = jnp.exp(g)
    x = k
    for d in range(s):
        if d > 0:
            x3 = x.reshape(c // s, s, kw)
            x = pltpu.roll(x3, (s - 1) if rev else 1, axis=1).reshape(c, kw) * gam
        p_ref[d * c:(d + 1) * c, :] = (q * x).astype(bf16)
    red = _dot(p_ref[...], ones_bd_ref[...])
    yield
    for d in range(s):
        a = a + red[d * c:(d + 1) * c, :] * lag_ref[d]

    a_st = stack_heads(a.astype(bf16)) * bd4
    qs_st = stack_heads((q * jnp.exp(b)).astype(bf16)) * bd4
    o_st = _dot(a_st, vst) + _dot_nt(qs_st, sin_ref[...])
    yield
    for h in range(nh):
        o_ref[pl.ds(t0, c), h * GLA_DV:(h + 1) * GLA_DV] = o_st[h * c:(h + 1) * c, :]


def _gla_kernel(q_ref, k_ref, v_ref, gf_ref, gb_ref,
                trif_ref, rowsf_ref, samef_ref, lagf_ref, trib_ref, rowsb_ref, sameb_ref, lagb_ref,
                bd2_ref, bd4_ref, ones_bd_ref,
                out_ref,
                of_ref, ob_ref, stf_ref, stb_ref, p_ref, sin_ref, *, seq, rows):
    c = GLA_CHUNK
    stf_ref[...] = jnp.zeros_like(stf_ref)
    stb_ref[...] = jnp.zeros_like(stb_ref)

    consts = (bd2_ref, bd4_ref, ones_bd_ref)
    n_chunks = seq // c

    def scan(i, carry):
        chains = []
        for u in range(GLA_UNROLL):
            n = i * GLA_UNROLL + u
            tf = pl.multiple_of(n * c, c)
            tb = pl.multiple_of((n_chunks - 1 - n) * c, c)
            chains.append(_gla_chunk(tf, False, q_ref, k_ref, v_ref, gf_ref, stf_ref, of_ref,
                                     p_ref.at[2 * u], sin_ref.at[2 * u],
                                     trif_ref, rowsf_ref, samef_ref, lagf_ref, *consts))
            chains.append(_gla_chunk(tb, True, q_ref, k_ref, v_ref, gb_ref, stb_ref, ob_ref,
                                     p_ref.at[2 * u + 1], sin_ref.at[2 * u + 1],
                                     trib_ref, rowsb_ref, sameb_ref, lagb_ref, *consts))
        _interleave(*chains)
        return carry

    lax.fori_loop(0, n_chunks // GLA_UNROLL, scan, 0)

    def finish(i, carry):
        r0 = pl.multiple_of(i * rows, rows)
        o = of_ref[pl.ds(r0, rows), :] + ob_ref[pl.ds(r0, rows), :]
        out_ref[0, pl.ds(r0, rows), :] = o.astype(out_ref.dtype)
        return carry

    lax.fori_loop(0, seq // rows, finish, 0)


def _gla(q, k, v, gf, gb):
    bsz, seq, _ = q.shape
    c, s, h = GLA_CHUNK, GLA_SUB, GLA_HEADS
    consts = []
    for rev in (False, True):
        tri, rows, same, lag = _gla_constants(rev)
        consts += [jnp.asarray(tri, bf16), jnp.asarray(rows), jnp.asarray(same), jnp.asarray(lag)]
    consts += [jnp.asarray(_block_diag(2, c, GLA_DK), bf16),
               jnp.asarray(_block_diag(h, c, GLA_DK), bf16),
               jnp.asarray(_block_diag(h, GLA_DK, c), bf16)]
    per_b = lambda w: pl.BlockSpec((1, seq, w), lambda b: (b, 0, 0))
    return pl.pallas_call(
        functools.partial(_gla_kernel, seq=seq, rows=256),
        grid=(bsz,),
        in_specs=[per_b(GLA_KW), per_b(GLA_KW), per_b(GLA_WIDTH), per_b(GLA_KW), per_b(GLA_KW)]
        + [_const_spec(x.shape) for x in consts],
        out_specs=per_b(GLA_WIDTH),
        out_shape=jax.ShapeDtypeStruct((bsz, seq, GLA_WIDTH), bf16),
        scratch_shapes=[pltpu.VMEM((seq, GLA_WIDTH), f32), pltpu.VMEM((seq, GLA_WIDTH), f32),
                        pltpu.VMEM((GLA_DV, GLA_KW), f32), pltpu.VMEM((GLA_DV, GLA_KW), f32),
                        pltpu.VMEM((2 * GLA_UNROLL, s * c, GLA_KW), bf16),
                        pltpu.VMEM((2 * GLA_UNROLL, GLA_DV, GLA_KW), bf16)],
        compiler_params=_cparams("parallel"),
        name="gla",
    )(q, k, v, gf, gb, *consts)


def _short_conv_phases(u, w, axis):
    nph = len(u)
    n = u[0].shape[axis]
    m = lax.broadcasted_iota(jnp.int32, u[0].shape, axis)
    before_first = jnp.where(m == 0, 0.0, pltpu.roll(u[nph - 1], 1, axis=axis))
    after_last = jnp.where(m == n - 1, 0.0, pltpu.roll(u[0], n - 1, axis=axis))
    out = []
    for r in range(nph):
        prev = u[r - 1] if r > 0 else before_first
        nxt = u[r + 1] if r < nph - 1 else after_last
        out.append(prev * w[0] + u[r] * w[1] + nxt * w[2] + w[3])
    return out


def _hyena_kernel(v_ref, x1_ref, x2_ref, cwt_v_ref, cwt_x1_ref, cw_x2_ref, a_ref, b_ref, corr_ref,
                  wre_ref, wim_ref, out_ref, src_ref, acc1_ref, acc2_ref, u_ref, y_ref, tok_ref):
    order = pl.program_id(2)
    ft = pl.program_id(3)
    n_ft = pl.num_programs(3)
    nb, nph, cb, ls = v_ref.shape
    rows = nb * cb

    rb = HY_RB

    def conv_to_src(u_ref_in, w_ref, gate_ref):
        def body(i, carry):
            bi = i // (cb // rb)
            crow = pl.multiple_of((i % (cb // rb)) * rb, rb)
            row0 = pl.multiple_of(i * rb, rb)
            w = w_ref[pl.ds(crow, rb), :]
            u = [u_ref_in[bi, r, pl.ds(crow, rb), :].astype(f32) for r in range(nph)]
            c = _short_conv_phases(u, [w[:, j:j + 1] for j in range(4)], axis=1)
            for r in range(nph):
                val = c[r] if gate_ref is None else c[r] * gate_ref[r, pl.ds(row0, rb), :]
                src_ref[r, pl.ds(row0, rb), :] = val.astype(bf16)
            return carry
        lax.fori_loop(0, rows // rb, body, 0, unroll=4)

    @pl.when(jnp.logical_and(order == 0, ft == 0))
    def _():
        conv_to_src(v_ref, cwt_v_ref, None)
        acc1_ref[...] = jnp.zeros_like(acc1_ref)

    @pl.when(jnp.logical_and(order == 1, ft == 0))
    def _():
        conv_to_src(x1_ref, cwt_x1_ref, acc1_ref)
        acc2_ref[...] = jnp.zeros_like(acc2_ref)

    wre = wre_ref[0]
    wim = wim_ref[0]
    ftile = wre.shape[1]
    for r in range(nph):
        u_ref[2 * r] = _dot(src_ref[r], wre).astype(bf16)
        u_ref[2 * r + 1] = _dot(src_ref[r], wim).astype(bf16)
    first_tile = (ft == 0).astype(f32).astype(bf16)

    rb = HY_RB

    def combine(i, carry):
        row0 = pl.multiple_of(i * rb, rb)
        crow = pl.multiple_of((i % (cb // rb)) * rb, rb)
        for lane0 in range(0, ftile, LANES):
            cols = slice(lane0, lane0 + LANES)
            u_re = [u_ref[2 * r, pl.ds(row0, rb), cols] for r in range(nph)]
            u_im = [u_ref[2 * r + 1, pl.ds(row0, rb), cols] for r in range(nph)]
            for rp in range(nph):
                y_re = y_im = None
                for r in range(nph):
                    d = rp - r + nph - 1
                    a = a_ref[0, 0, 0, d, pl.ds(crow, rb), cols]
                    b = b_ref[0, 0, 0, d, pl.ds(crow, rb), cols]
                    t_re = u_re[r] * a - u_im[r] * b
                    t_im = u_re[r] * b + u_im[r] * a
                    if lane0 == 0:
                        t_im = t_im + u_im[r] * (corr_ref[0, 0, d, pl.ds(crow, rb), :] * first_tile)
                    y_re = t_re if y_re is None else y_re + t_re
                    y_im = t_im if y_im is None else y_im + t_im
                y_ref[2 * rp, pl.ds(row0, rb), cols] = y_re
                y_ref[2 * rp + 1, pl.ds(row0, rb), cols] = y_im
        return carry

    lax.fori_loop(0, rows // rb, combine, 0)

    @pl.when(order == 0)
    def _():
        for rp in range(nph):
            acc1_ref[rp] += _dot_nt(y_ref[2 * rp], wre) + _dot_nt(y_ref[2 * rp + 1], wim)

    @pl.when(order == 1)
    def _():
        for rp in range(nph):
            acc2_ref[rp] += _dot_nt(wre, y_ref[2 * rp]) + _dot_nt(wim, y_ref[2 * rp + 1])

    @pl.when(jnp.logical_and(order == 1, ft == n_ft - 1))
    def _():
        w = cw_x2_ref[...]
        taps = [w[i:i + 1] for i in range(4)]
        for i in range(nb):
            x2 = _short_conv_phases([x2_ref[i, r, 0].astype(f32) for r in range(nph)], taps, axis=0)
            for r in range(nph):
                y = x2[r] * acc2_ref[r][:, i * cb:(i + 1) * cb]
                for c in range(cb // LANES):
                    tok_ref[c, pl.ds(r, ls, stride=nph), :] = y[:, c * LANES:(c + 1) * LANES]
            for c in range(cb // LANES):
                out_ref[i, 0, :, c * LANES:(c + 1) * LANES] = tok_ref[c].astype(out_ref.dtype)


def _hyena(vx1t, x2, cwt, cw_x2, fa, fb, fcorr, ws):
    bsz, nph, two_w, ls = vx1t.shape
    width = two_w // 2
    cb, nb, ftile = HY_CBLK, HY_BBLK, HY_FTILE
    n_cb = width // cb
    n_ft = ls // ftile
    nd = fa.shape[3]
    rows = nb * cb
    filt = lambda w: pl.BlockSpec((1, 1, 1, nd, cb, w), lambda c, b, o, f: (o, c, f, 0, 0, 0))
    return pl.pallas_call(
        _hyena_kernel,
        grid=(n_cb, bsz // nb, HY_ORDER, n_ft),
        in_specs=[pl.BlockSpec((nb, nph, cb, ls), lambda c, b, o, f: (b, 0, c, 0)),
                  pl.BlockSpec((nb, nph, cb, ls), lambda c, b, o, f: (b, 0, n_cb + c, 0)),
                  pl.BlockSpec((nb, nph, 1, ls, cb), lambda c, b, o, f: (b, 0, c, 0, 0)),
                  pl.BlockSpec((cb, 4), lambda c, b, o, f: (c, 0)),
                  pl.BlockSpec((cb, 4), lambda c, b, o, f: (n_cb + c, 0)),
                  pl.BlockSpec((4, cb), lambda c, b, o, f: (0, c)),
                  filt(ftile), filt(ftile),
                  pl.BlockSpec((1, 1, nd, cb, LANES), lambda c, b, o, f: (o, c, 0, 0, 0)),
                  pl.BlockSpec((1, ls, ftile), lambda c, b, o, f: (f, 0, 0)),
                  pl.BlockSpec((1, ls, ftile), lambda c, b, o, f: (n_ft + f, 0, 0))],
        out_specs=pl.BlockSpec((nb, 1, nph * ls, cb), lambda c, b, o, f: (b, c, 0, 0)),
        out_shape=jax.ShapeDtypeStruct((bsz, n_cb, nph * ls, cb), bf16),
        scratch_shapes=[pltpu.VMEM((nph, rows, ls), bf16), pltpu.VMEM((nph, rows, ls), f32),
                        pltpu.VMEM((nph, ls, rows), f32),
                        pltpu.VMEM((2 * nph, rows, ftile), bf16), pltpu.VMEM((2 * nph, rows, ftile), bf16),
                        pltpu.VMEM((cb // LANES, nph * ls, LANES), f32)],
        compiler_params=_cparams("parallel", "parallel", "arbitrary", "arbitrary"),
        name="hyena",
    )(vx1t, vx1t, x2, cwt, cwt, cw_x2, fa, fb, fcorr, ws, ws)


def _out_ffn_kernel(x_ref, og_ref, r_ref, hy_ref, gnw_ref, hnw_ref, wo_g_ref, wo_h_ref, fnw_ref, wg_ref, wu_ref,
                    wd_ref, *rest, final_norm):
    if final_norm:
        lnw_ref, out_ref = rest
    else:
        (out_ref,) = rest
    o = og_ref[...].astype(f32)
    gate = _silu(r_ref[...].astype(f32))
    gnw = gnw_ref[...]
    og = jnp.concatenate([_rms(o[:, h * GLA_DV:(h + 1) * GLA_DV]) * gnw for h in range(GLA_HEADS)], axis=1)
    og = (og * gate).astype(bf16)
    hy = jnp.concatenate([hy_ref[0, ci] for ci in range(hy_ref.shape[1])], axis=1)
    hy = (_rms(hy.astype(f32)) * hnw_ref[...]).astype(bf16)
    x = x_ref[...] + _dot(og, wo_g_ref[...]) + _dot(hy, wo_h_ref[...])
    h = (_rms(x) * fnw_ref[...]).astype(bf16)
    ff = (_silu(_dot(h, wg_ref[...])) * _dot(h, wu_ref[...])).astype(bf16)
    x = x + _dot(ff, wd_ref[...])
    if final_norm:
        x = _rms(x) * lnw_ref[...]
    out_ref[...] = x


def _out_ffn(x, og, r, hy, gnw, hnw, wo_g, wo_h, fnw, wg, wu, wd, lnw, *, tm):
    n, d = x.shape
    _, n_cb, seq, cb = hy.shape
    tiles_per_b = seq // tm
    row = lambda w: pl.BlockSpec((tm, w), lambda i: (i, 0))
    weights = [gnw, hnw, wo_g, wo_h, fnw, wg, wu, wd] + ([] if lnw is None else [lnw])
    return pl.pallas_call(
        functools.partial(_out_ffn_kernel, final_norm=lnw is not None),
        grid=(n // tm,),
        in_specs=[row(d), row(og.shape[1]), row(r.shape[1]),
                  pl.BlockSpec((1, n_cb, tm, cb), lambda i: (i // tiles_per_b, 0, i % tiles_per_b, 0))]
        + [_const_spec(w.shape) for w in weights],
        out_specs=row(d),
        out_shape=jax.ShapeDtypeStruct((n, d), f32),
        compiler_params=_cparams("parallel"),
        name="out_ffn",
    )(x, og, r, hy, *weights)


def _prepare_layer(norm_mix, w_in, wg_f, bg_f, wg_b, bg_b, gla_norm, conv_w, conv_b, hy_norm, w_out,
                   norm_ffn, w_gate, w_up, w_down):
    d = w_in.shape[0]
    hw = hy_norm.shape[0]
    o_lr = 2 * GLA_KW + 2 * GLA_WIDTH
    o_hy = o_lr + 2 * GLA_GATE_RANK
    wq = w_in[:, :GLA_KW] * (GLA_DK ** -0.5)
    wa = jnp.concatenate([wq, w_in[:, GLA_KW:o_lr], w_in[:, o_hy + 2 * hw:]], axis=1).astype(bf16)
    wlr = jnp.zeros((d, LANES), f32).at[:, :2 * GLA_GATE_RANK].set(w_in[:, o_lr:o_hy]).astype(bf16)
    wt = w_in[:, o_hy:o_hy + 2 * hw].T.astype(bf16)
    wg = jnp.zeros((LANES, 2 * GLA_KW), f32)
    wg = wg.at[:GLA_GATE_RANK, :GLA_KW].set(wg_f).at[GLA_GATE_RANK:2 * GLA_GATE_RANK, GLA_KW:].set(wg_b)
    wg = wg.astype(bf16)
    bg = jnp.concatenate([bg_f, bg_b])[None, :]
    cw = jnp.concatenate([conv_w, conv_b[None, :]], axis=0)
    return dict(
        nw=norm_mix[None, :], wa=wa, wlr=wlr, wt=wt, wgate=wg, bg=bg, gnw=gla_norm[None, :],
        cwt=cw[:, :2 * hw].T, cw_x2=cw[:, 2 * hw:], hnw=hy_norm[None, :],
        wo_g=w_out[:GLA_WIDTH].astype(bf16), wo_h=w_out[GLA_WIDTH:].astype(bf16),
        fnw=norm_ffn[None, :], wg=w_gate.astype(bf16), wu=w_up.astype(bf16), wd=w_down.astype(bf16))


def _layer(x, p, filt, wf, lnw):
    q, k, v, r, gf, gb, x2, vx1t = _in_proj(x, p["nw"], p["wa"], p["wlr"], p["wt"], p["wgate"], p["bg"],
                                            tm=512)
    og = _gla(q, k, v, gf, gb)
    hy = _hyena(vx1t, x2, p["cwt"], p["cw_x2"], *filt, wf)
    bsz, seq, d = x.shape
    n = bsz * seq
    y = _out_ffn(x.reshape(n, d), og.reshape(n, -1), r.reshape(n, -1), hy, p["gnw"], p["hnw"],
                 p["wo_g"], p["wo_h"], p["fnw"], p["wg"], p["wu"], p["wd"], lnw, tm=512)
    return y.reshape(bsz, seq, d)


def kernel(x_prompt, x_sample, norm_mix, w_in, gla_wg_f, gla_bg_f, gla_wg_b, gla_bg_b, gla_norm, hy_conv_w, hy_conv_b, hy_w1, hy_b1, hy_freq1, hy_w2, hy_b2, hy_freq2, hy_w3, hy_skip, hy_norm, w_out, norm_ffn, w_gate, w_up, w_down, norm_final):
    depth = w_in.shape[0]
    seq = x_prompt.shape[1]
    assert x_sample.shape[1] == seq
    width = hy_norm.shape[1]
    wf = _dft_matrix(seq // HY_PHASES)
    feats, lag = _positional_features(seq)
    layers, filters = [], []
    for l in range(depth):
        layers.append(_prepare_layer(norm_mix[l], w_in[l], gla_wg_f[l], gla_bg_f[l], gla_wg_b[l], gla_bg_b[l],
                                     gla_norm[l], hy_conv_w[l], hy_conv_b[l], hy_norm[l], w_out[l],
                                     norm_ffn[l], w_gate[l], w_up[l], w_down[l]))
        filters.append(_hyena_filters(feats, lag, hy_w1[l], hy_b1[l], hy_freq1[l], hy_w2[l], hy_b2[l], hy_freq2[l],
                                      hy_w3[l], hy_skip[l], wf, seq=seq, width=width))
    outs = []
    for x in (x_prompt, x_sample):
        for l in range(depth):
            x = _layer(x, layers[l], filters[l], wf, norm_final[None, :] if l == depth - 1 else None)
        outs.append(x)
    return tuple(outs)
```
